```python
import jax, jax.numpy as jnp
from jax import lax
import numpy as np

D_MODEL = 2048
BATCH = 8
SEQ = 2048
DEPTH = 1
DEC_BATCH = 16
DEC_SEQ = 2048
PAST_LEN = 128

HEAD_DIM = 128
EPS = 1e-6
QBLOCK = 128
GRID_W = 64
A_GROUPS = ((128, 1), (512, 4), (2048, 16))
A_HEADS_PER_GROUP = 4
A_HEADS = A_HEADS_PER_GROUP * len(A_GROUPS)
A_WIDTH = A_HEADS * HEAD_DIM
A_OUT = A_HEADS_PER_GROUP * HEAD_DIM
ROPE_THETA_A = 500000.0
ROT_A = HEAD_DIM // 4
B_Q_HEADS = 8
B_KV_HEADS = 2
B_Q_WIDTH = B_Q_HEADS * HEAD_DIM
B_KV_WIDTH = B_KV_HEADS * HEAD_DIM
ROPE_THETA_B = 10000.0
AXIAL_HALF = HEAD_DIM // 2
IN_SPLITS = (A_WIDTH, A_WIDTH, A_WIDTH, B_Q_WIDTH, B_KV_WIDTH, B_KV_WIDTH, D_MODEL, D_MODEL)
IN_WIDTH = sum(IN_SPLITS)
PEER_HEADS = 8
PEER_NKEYS = 128
PEER_EXPERTS = PEER_NKEYS * PEER_NKEYS
PEER_QDIM = 256
PEER_TOPK = 16
PEER_CHUNK = 128

kernel_name = "hybrid_dilated_axial_peer_encoder"


def rms_norm(x, g):
    xf = x.astype(jnp.float32)
    y = xf * lax.rsqrt(jnp.mean(xf * xf, axis=-1, keepdims=True) + EPS)
    return (y * g.astype(jnp.float32)).astype(x.dtype)


def rope_cos_sin(pos, dim, theta):
    inv = theta ** (-(jnp.arange(0, dim, 2, dtype=jnp.float32) / dim))
    ang = pos.astype(jnp.float32)[:, None] * inv[None, :]
    return jnp.cos(ang), jnp.sin(ang)


def apply_rope(x, cos, sin):
    xf = x.astype(jnp.float32)
    x1, x2 = jnp.split(xf, 2, axis=-1)
    c = cos[None, :, None, :]
    s = sin[None, :, None, :]
    return jnp.concatenate([x1 * c - x2 * s, x2 * c + x1 * s], axis=-1).astype(x.dtype)


def banded_attention(q, k, v, half):
    n, length, nh, dh = q.shape
    nb = -(-length // QBLOCK)
    lp = nb * QBLOCK
    kw = QBLOCK + 2 * half
    qp = jnp.pad(q, ((0, 0), (0, lp - length), (0, 0), (0, 0)))
    kp = jnp.pad(k, ((0, 0), (half, lp - length + half), (0, 0), (0, 0)))
    vp = jnp.pad(v, ((0, 0), (half, lp - length + half), (0, 0), (0, 0)))
    kidx = jnp.arange(nb)[:, None] * QBLOCK + jnp.arange(kw)[None, :]
    kb = kp[:, kidx]
    vb = vp[:, kidx]
    qb = qp.reshape(n, nb, QBLOCK, nh, dh)
    s = jnp.einsum('nbqhd,nbkhd->nbhqk', qb, kb).astype(jnp.float32) * (dh ** -0.5)
    key_pos = kidx - half
    q_pos = jnp.arange(nb)[:, None] * QBLOCK + jnp.arange(QBLOCK)[None, :]
    rel = key_pos[:, None, :] - q_pos[:, :, None]
    key_ok = (key_pos[:, None, :] >= 0) & (key_pos[:, None, :] < length)
    mask = (jnp.abs(rel) <= half) & (key_ok | (q_pos[:, :, None] >= length))
    s = jnp.where(mask[None, :, None], s, -jnp.inf)
    lse = jax.nn.logsumexp(s, axis=-1)
    p = jnp.exp(s - lse[..., None]).astype(v.dtype)
    o = jnp.einsum('nbhqk,nbkhd->nbqhd', p, vb).reshape(n, lp, nh, dh)[:, :length]
    lse = lse.transpose(0, 1, 3, 2).reshape(n, lp, nh)[:, :length]
    return o, lse


def dilated_group(q, k, v, window, dilation):
    bn, length, nh, dh = q.shape
    ls = length // dilation
    half = (window // 2) // dilation

    def to_res(t):
        return t.reshape(bn, ls, dilation, nh, dh).transpose(0, 2, 1, 3, 4).reshape(bn * dilation, ls, nh, dh)

    o, lse = banded_attention(to_res(q), to_res(k), to_res(v), half)
    o = o.reshape(bn, dilation, ls, nh, dh).transpose(0, 2, 1, 3, 4).reshape(bn, length, nh, dh)
    lse = lse.reshape(bn, dilation, ls, nh).transpose(0, 2, 1, 3).reshape(bn, length, nh)
    return o, lse


def gqa_dense_blocked(q, k, v):
    bn, length, hq, dh = q.shape
    hkv = k.shape[2]
    grp = hq // hkv
    nb = length // QBLOCK
    qb = q.reshape(bn, nb, QBLOCK, hkv, grp, dh).transpose(1, 0, 2, 3, 4, 5)
    scale = dh ** -0.5

    def one_block(qblk):
        s = jnp.einsum('bqkgd,bskd->bkgqs', qblk, k).astype(jnp.float32) * scale
        p = jax.nn.softmax(s, axis=-1).astype(v.dtype)
        return jnp.einsum('bkgqs,bskd->bqkgd', p, v)

    o = lax.map(one_block, qb)
    return o.transpose(1, 0, 2, 3, 4, 5).reshape(bn, length, hq * dh)


def peer(h, w_pq, sub_keys, u_emb, v_emb):
    bn, length, d = h.shape
    t = bn * length
    ht = h.reshape(t, d)
    q = (ht @ w_pq).reshape(t, PEER_HEADS, 2, PEER_QDIM // 2)
    s = jnp.einsum('thcd,hcnd->thcn', q, sub_keys).astype(jnp.float32)
    v1, i1 = lax.top_k(s[:, :, 0], PEER_TOPK)
    v2, i2 = lax.top_k(s[:, :, 1], PEER_TOPK)
    cand = (v1[..., :, None] + v2[..., None, :]).reshape(t, PEER_HEADS, PEER_TOPK * PEER_TOPK)
    cv, ci = lax.top_k(cand, PEER_TOPK)
    e = (jnp.take_along_axis(i1, ci // PEER_TOPK, axis=-1) * PEER_NKEYS
         + jnp.take_along_axis(i2, ci % PEER_TOPK, axis=-1))
    g = jax.nn.softmax(cv, axis=-1)
    nc = t // PEER_CHUNK
    sel = PEER_HEADS * PEER_TOPK

    def chunk(args):
        hc, ec, gc = args
        a = jnp.einsum('ckd,cd->ck', u_emb[ec], hc)
        coef = (jax.nn.gelu(a.astype(jnp.float32), approximate=False) * gc).astype(v_emb.dtype)
        return jnp.einsum('ck,ckd->cd', coef, v_emb[ec])

    out = lax.map(chunk, (ht.reshape(nc, PEER_CHUNK, d), e.reshape(nc, PEER_CHUNK, sel),
                          g.reshape(nc, PEER_CHUNK, sel)))
    return out.reshape(bn, length, d).astype(h.dtype)


def encoder_layer(x, g_mix, w_in, qn_a, kn_a, qn_b, kn_b, w_br_a, w_br_b, w_out,
                  g_ffn, w_pq, sub_keys, u_emb, v_emb):
    bn, length, _ = x.shape
    h = rms_norm(x, g_mix)
    proj = h @ w_in
    cuts = [int(c) for c in np.cumsum(IN_SPLITS)[:-1]]
    qa, ka, va, qb, kb, vb, ga, gb = jnp.split(proj, cuts, axis=-1)

    qa = rms_norm(qa.reshape(bn, length, A_HEADS, HEAD_DIM), qn_a)
    ka = rms_norm(ka.reshape(bn, length, A_HEADS, HEAD_DIM), kn_a)
    va = va.reshape(bn, length, A_HEADS, HEAD_DIM)
    cos_a, sin_a = rope_cos_sin(jnp.arange(length), ROT_A, ROPE_THETA_A)
    qa = jnp.concatenate([apply_rope(qa[..., :ROT_A], cos_a, sin_a), qa[..., ROT_A:]], axis=-1)
    ka = jnp.concatenate([apply_rope(ka[..., :ROT_A], cos_a, sin_a), ka[..., ROT_A:]], axis=-1)
    outs, lses = [], []
    for gi, (window, dilation) in enumerate(A_GROUPS):
        hs = slice(gi * A_HEADS_PER_GROUP, (gi + 1) * A_HEADS_PER_GROUP)
        o_g, l_g = dilated_group(qa[:, :, hs], ka[:, :, hs], va[:, :, hs], window, dilation)
        outs.append(o_g)
        lses.append(l_g)
    wts = jax.nn.softmax(jnp.stack(lses, axis=0), axis=0)
    oa = jnp.sum(wts[..., None].astype(va.dtype) * jnp.stack(outs, axis=0), axis=0)
    oa = oa.reshape(bn, length, A_OUT)

    rows = length // GRID_W
    row_ids = jnp.repeat(jnp.arange(rows), GRID_W)
    col_ids = jnp.tile(jnp.arange(GRID_W), rows)
    cos_r, sin_r = rope_cos_sin(row_ids, AXIAL_HALF, ROPE_THETA_B)
    cos_c, sin_c = rope_cos_sin(col_ids, AXIAL_HALF, ROPE_THETA_B)

    def axial(t):
        return jnp.concatenate([apply_rope(t[..., :AXIAL_HALF], cos_r, sin_r),
                                apply_rope(t[..., AXIAL_HALF:], cos_c, sin_c)], axis=-1)

    qb = axial(rms_norm(qb.reshape(bn, length, B_Q_HEADS, HEAD_DIM), qn_b))
    kb = axial(rms_norm(kb.reshape(bn, length, B_KV_HEADS, HEAD_DIM), kn_b))
    vb = vb.reshape(bn, length, B_KV_HEADS, HEAD_DIM)
    ob = gqa_dense_blocked(qb, kb, vb)

    merged = jax.nn.sigmoid(ga) * (oa @ w_br_a) + jax.nn.sigmoid(gb) * (ob @ w_br_b)
    x = x + merged @ w_out

    x = x + peer(rms_norm(x, g_ffn), w_pq, sub_keys, u_emb, v_emb)
    return x


def setup_inputs(seed: int = 0) -> dict:
    key = jax.random.key(seed)
    ks = jax.random.split(key, 16)
    f32 = jnp.float32

    def nrm(k, shape, scale):
        return jax.random.normal(k, shape, f32) * scale

    def gain(k, shape):
        return 1.0 + 0.02 * jax.random.normal(k, shape, f32)

    return {
        "x_prompt": jax.random.normal(ks[0], (BATCH, SEQ, D_MODEL), f32),
        "x_sample": jax.random.normal(ks[1], (DEC_BATCH, DEC_SEQ, D_MODEL), f32),
        "g_mix": gain(ks[2], (DEPTH, D_MODEL)),
        "w_in": nrm(ks[3], (DEPTH, D_MODEL, IN_WIDTH), D_MODEL ** -0.5),
        "qn_a": gain(ks[4], (DEPTH, HEAD_DIM)),
        "kn_a": gain(ks[5], (DEPTH, HEAD_DIM)),
        "qn_b": gain(ks[6], (DEPTH, HEAD_DIM)),
        "kn_b": gain(ks[7], (DEPTH, HEAD_DIM)),
        "w_br_a": nrm(ks[8], (DEPTH, A_OUT, D_MODEL), A_OUT ** -0.5),
        "w_br_b": nrm(ks[9], (DEPTH, B_Q_WIDTH, D_MODEL), B_Q_WIDTH ** -0.5),
        "w_out": nrm(ks[10], (DEPTH, D_MODEL, D_MODEL), D_MODEL ** -0.5),
        "g_ffn": gain(ks[11], (DEPTH, D_MODEL)),
        "w_pq": nrm(ks[12], (DEPTH, D_MODEL, PEER_HEADS * PEER_QDIM), D_MODEL ** -0.5),
        "sub_keys": nrm(ks[13], (DEPTH, PEER_HEADS, 2, PEER_NKEYS, PEER_QDIM // 2), (PEER_QDIM // 2) ** -0.5),
        "u_emb": nrm(ks[14], (DEPTH, PEER_EXPERTS, D_MODEL), D_MODEL ** -0.5),
        "v_emb": nrm(ks[15], (DEPTH, PEER_EXPERTS, D_MODEL), PEER_HEADS ** -0.5),
    }


def reference(x_prompt, x_sample, g_mix, w_in, qn_a, kn_a, qn_b, kn_b, w_br_a, w_br_b, w_out,
              g_ffn, w_pq, sub_keys, u_emb, v_emb):
    y_prompt = x_prompt
    y_sample = x_sample
    for l in range(DEPTH):
        y_prompt = encoder_layer(y_prompt, g_mix[l], w_in[l], qn_a[l], kn_a[l], qn_b[l], kn_b[l],
                                 w_br_a[l], w_br_b[l], w_out[l], g_ffn[l], w_pq[l], sub_keys[l],
                                 u_emb[l], v_emb[l])
        y_sample = encoder_layer(y_sample, g_mix[l], w_in[l], qn_a[l], kn_a[l], qn_b[l], kn_b[l],
                                 w_br_a[l], w_br_b[l], w_out[l], g_ffn[l], w_pq[l], sub_keys[l],
                                 u_emb[l], v_emb[l])
    return (y_prompt, y_sample)
```

```python
import functools
import math

import jax
import jax.numpy as jnp
from jax import lax
from jax.experimental import pallas as pl
from jax.experimental.pallas import tpu as pltpu

F32 = jnp.float32
BF16 = jnp.bfloat16

D_MODEL = 2048
SEQ = 2048
HEAD_DIM = 128
EPS = 1e-6
GRID_W = 64
SCALE = HEAD_DIM ** -0.5
A_HEADS_PER_GROUP = 4
A_DILATIONS = (1, 4, 16)
A_HALF = 64
ROPE_THETA_A = 500000.0
ROT_A = HEAD_DIM // 4
ROPE_THETA_B = 10000.0
AXIAL_HALF = HEAD_DIM // 2
QKV_WIDTH = 6144
QA0, KA0, VA0, QB0, KB0, VB0 = 0, 12, 24, 36, 44, 46
N_QKV_HEADS = 48
PEER_HEADS = 8
PEER_NKEYS = 128
PEER_TOPK = 16
PEER_SEL = PEER_HEADS * PEER_TOPK
ROW_CHUNKS = D_MODEL // 128
NEG = -1e30
INV_SQRT2 = 0.7071067811865476

LANES = 128
VMEM_LIMIT = 56 * 1024 * 1024


def _to_rows(x2):
    n = x2.shape[0]
    return x2.reshape(n // 8, 8, ROW_CHUNKS, LANES).transpose(0, 2, 1, 3).reshape(n * ROW_CHUNKS, LANES)


def _from_rows(r2, n):
    return r2.reshape(n // 8, ROW_CHUNKS, 8, LANES).transpose(0, 2, 1, 3).reshape(n, ROW_CHUNKS * LANES)


def _row_slice(i):
    return pl.ds((i >> 3) * (8 * ROW_CHUNKS) + (i & 7), ROW_CHUNKS, stride=8)


PACK_BLOCKS = 8


def _pack_table_kernel(u_ref, v_ref, o_ref):
    for e in range(8 * PACK_BLOCKS):
        o_ref[e, 0:ROW_CHUNKS, :] = u_ref[_row_slice(e), :].astype(BF16)
        o_ref[e, ROW_CHUNKS:2 * ROW_CHUNKS, :] = v_ref[_row_slice(e), :].astype(BF16)


def _pack_table(u_emb, v_emb):
    n = u_emb.shape[0]
    experts = 8 * PACK_BLOCKS
    rows = experts * ROW_CHUNKS
    return pl.pallas_call(
        _pack_table_kernel,
        grid=(n // experts,),
        in_specs=[pl.BlockSpec((rows, LANES), lambda i: (i, 0)),
                  pl.BlockSpec((rows, LANES), lambda i: (i, 0))],
        out_specs=pl.BlockSpec((experts, 2 * ROW_CHUNKS, LANES), lambda i: (i, 0, 0)),
        out_shape=jax.ShapeDtypeStruct((n, 2 * ROW_CHUNKS, LANES), BF16),
        name="pack_table",
    )(_to_rows(u_emb), _to_rows(v_emb))


def _rope_tables():
    pos = jnp.arange(SEQ, dtype=F32)
    ha = ROT_A // 2
    inv_a = ROPE_THETA_A ** (-(jnp.arange(0, ROT_A, 2, dtype=F32) / ROT_A))
    ang = pos[:, None] * inv_a[None, :]
    cos, sin = jnp.cos(ang), jnp.sin(ang)
    pad = jnp.zeros((SEQ, HEAD_DIM - ROT_A), F32)
    zh = jnp.zeros((SEQ, ha), F32)
    ca = jnp.concatenate([cos, cos, pad + 1.0], axis=1)
    s1a = jnp.concatenate([-sin, zh, pad], axis=1)
    s2a = jnp.concatenate([zh, sin, pad], axis=1)
    rows = SEQ // GRID_W
    row_ids = jnp.repeat(jnp.arange(rows), GRID_W).astype(F32)
    col_ids = jnp.tile(jnp.arange(GRID_W), rows).astype(F32)
    inv_b = ROPE_THETA_B ** (-(jnp.arange(0, AXIAL_HALF, 2, dtype=F32) / AXIAL_HALF))
    ar = row_ids[:, None] * inv_b[None, :]
    ac = col_ids[:, None] * inv_b[None, :]
    zq = jnp.zeros_like(ar)
    cb = jnp.concatenate([jnp.cos(ar), jnp.cos(ar), jnp.cos(ac), jnp.cos(ac)], axis=1)
    s1b = jnp.concatenate([-jnp.sin(ar), zq, -jnp.sin(ac), zq], axis=1)
    s2b = jnp.concatenate([zq, jnp.sin(ar), zq, jnp.sin(ac)], axis=1)
    return ca, s1a, s2a, cb, s1b, s2b


IN_TM = 512
IN_TN = 256
IN_NJ = QKV_WIDTH // IN_TN


def _in_proj_kernel(x_ref, gmix_ref, w_ref, gain_ref, ca_ref, s1a_ref, s2a_ref,
                    cb_ref, s1b_ref, s2b_ref, o_ref, h_scr):
    j = pl.program_id(1)

    @pl.when(j == 0)
    def _():
        x = x_ref[...]
        ms = jnp.mean(x * x, axis=-1, keepdims=True)
        h_scr[...] = (x * lax.rsqrt(ms + EPS) * gmix_ref[...]).astype(BF16)

    t = jnp.dot(h_scr[...], w_ref[...], preferred_element_type=F32)
    is_a = j < 12
    is_b = jnp.logical_and(j >= 18, j < 23)

    def norm_rope(c_ref, s1_ref, s2_ref, sh):
        for hh in range(IN_TN // HEAD_DIM):
            y = t[:, hh * HEAD_DIM:(hh + 1) * HEAD_DIM]
            y = y * lax.rsqrt(jnp.mean(y * y, axis=-1, keepdims=True) + EPS) * gain_ref[0]
            out = (y * c_ref[...]
                   + pltpu.roll(y, HEAD_DIM - sh, 1) * s1_ref[...]
                   + pltpu.roll(y, sh, 1) * s2_ref[...])
            o_ref[hh] = out.astype(BF16)

    @pl.when(is_a)
    def _():
        norm_rope(ca_ref, s1a_ref, s2a_ref, ROT_A // 2)

    @pl.when(is_b)
    def _():
        norm_rope(cb_ref, s1b_ref, s2b_ref, AXIAL_HALF // 2)

    @pl.when(jnp.logical_not(jnp.logical_or(is_a, is_b)))
    def _():
        for hh in range(IN_TN // HEAD_DIM):
            o_ref[hh] = t[:, hh * HEAD_DIM:(hh + 1) * HEAD_DIM].astype(BF16)


def _in_proj(x2, gmix, w_qkv, gains, tables):
    t_tok = x2.shape[0]
    nseq = SEQ // IN_TM
    tab_spec = pl.BlockSpec((IN_TM, HEAD_DIM), lambda i, j: (i % nseq, 0))
    return pl.pallas_call(
        _in_proj_kernel,
        grid=(t_tok // IN_TM, IN_NJ),
        in_specs=[pl.BlockSpec((IN_TM, D_MODEL), lambda i, j: (i, 0)),
                  pl.BlockSpec((1, D_MODEL), lambda i, j: (0, 0)),
                  pl.BlockSpec((D_MODEL, IN_TN), lambda i, j: (0, j)),
                  pl.BlockSpec((1, 1, HEAD_DIM), lambda i, j: (j, 0, 0)),
                  tab_spec, tab_spec, tab_spec, tab_spec, tab_spec, tab_spec],
        out_specs=pl.BlockSpec((IN_TN // HEAD_DIM, IN_TM, HEAD_DIM), lambda i, j: (j, i, 0)),
        out_shape=jax.ShapeDtypeStruct((N_QKV_HEADS, t_tok, HEAD_DIM), BF16),
        scratch_shapes=[pltpu.VMEM((IN_TM, D_MODEL), BF16)],
        compiler_params=pltpu.CompilerParams(dimension_semantics=("arbitrary", "arbitrary"),
                                             vmem_limit_bytes=VMEM_LIMIT),
        name="in_proj",
    )(x2, gmix, w_qkv, gains, *tables)


ATT_TQ = 256
NT = (((1,), (1,)), ((), ()))


def _attn_b_kernel(q_ref, k_ref, v_ref, o_ref):
    k = k_ref[0]
    v = v_ref[0]
    for hh in range(4):
        s = lax.dot_general(q_ref[hh], k, NT, preferred_element_type=F32) * SCALE
        m = jnp.max(s, axis=-1, keepdims=True)
        p = jnp.exp(s - m)
        l = jnp.sum(p, axis=-1, keepdims=True)
        o = jnp.dot(p.astype(BF16), v, preferred_element_type=F32) / l
        o_ref[:, hh * HEAD_DIM:(hh + 1) * HEAD_DIM] = o.astype(BF16)


def _attn_b(qkv, n_batch):
    t_tok = qkv.shape[1]
    nq = SEQ // ATT_TQ
    return pl.pallas_call(
        _attn_b_kernel,
        grid=(n_batch, 2, nq),
        in_specs=[pl.BlockSpec((4, ATT_TQ, HEAD_DIM), lambda b, g, qi: (QB0 // 4 + g, b * nq + qi, 0)),
                  pl.BlockSpec((1, SEQ, HEAD_DIM), lambda b, g, qi: (KB0 + g, b, 0)),
                  pl.BlockSpec((1, SEQ, HEAD_DIM), lambda b, g, qi: (VB0 + g, b, 0))],
        out_specs=pl.BlockSpec((ATT_TQ, 4 * HEAD_DIM), lambda b, g, qi: (b * nq + qi, g)),
        out_shape=jax.ShapeDtypeStruct((t_tok, 8 * HEAD_DIM), BF16),
        compiler_params=pltpu.CompilerParams(
            dimension_semantics=("arbitrary", "arbitrary", "arbitrary"), vmem_limit_bytes=VMEM_LIMIT),
        name="attn_b",
    )(qkv, qkv, qkv)


def _key_window(dil):
    span = A_HALF * dil
    span = -(-span // LANES) * LANES
    return min(SEQ, ATT_TQ + 2 * span), span


def _attn_a_kernel(q0_ref, q1_ref, q2_ref, k0_ref, k1_ref, k2_ref, v0_ref, v1_ref, v2_ref, o_ref):
    q0pos = pl.program_id(2) * ATT_TQ
    scores, values = [], []
    for q_ref, k_ref, v_ref, dil in ((q0_ref, k0_ref, v0_ref, A_DILATIONS[0]),
                                     (q1_ref, k1_ref, v1_ref, A_DILATIONS[1]),
                                     (q2_ref, k2_ref, v2_ref, A_DILATIONS[2])):
        width, span = _key_window(dil)
        if width < SEQ:
            start = pl.multiple_of(jnp.clip(q0pos - span, 0, SEQ - width), LANES)
            kk = k_ref[0, pl.ds(start, width), :]
            vv = v_ref[0, pl.ds(start, width), :]
        else:
            start = 0
            kk = k_ref[0]
            vv = v_ref[0]
        s = lax.dot_general(q_ref[0], kk, NT, preferred_element_type=F32) * SCALE
        rel = (lax.broadcasted_iota(jnp.int32, (ATT_TQ, width), 1)
               - lax.broadcasted_iota(jnp.int32, (ATT_TQ, width), 0)) + (start - q0pos)
        valid = jnp.abs(rel) <= A_HALF * dil
        if dil > 1:
            valid = jnp.logical_and(valid, (rel & (dil - 1)) == 0)
        scores.append(jnp.where(valid, s, NEG))
        values.append(vv)
    m = functools.reduce(jnp.maximum, [jnp.max(s, axis=-1, keepdims=True) for s in scores])
    l = jnp.zeros((ATT_TQ, 1), F32)
    acc = jnp.zeros((ATT_TQ, HEAD_DIM), F32)
    for s, vv in zip(scores, values):
        p = jnp.exp(s - m)
        l = l + jnp.sum(p, axis=-1, keepdims=True)
        acc = acc + jnp.dot(p.astype(BF16), vv, preferred_element_type=F32)
    o_ref[...] = (acc / l).astype(BF16)


def _attn_a(qkv, n_batch):
    t_tok = qkv.shape[1]
    nq = SEQ // ATT_TQ
    g4 = A_HEADS_PER_GROUP

    def q_spec(g):
        return pl.BlockSpec((1, ATT_TQ, HEAD_DIM), lambda b, hh, qi: (QA0 + g4 * g + hh, b * nq + qi, 0))

    def kv_spec(base, g):
        return pl.BlockSpec((1, SEQ, HEAD_DIM), lambda b, hh, qi: (base + g4 * g + hh, b, 0))

    return pl.pallas_call(
        _attn_a_kernel,
        grid=(n_batch, g4, nq),
        in_specs=[q_spec(0), q_spec(1), q_spec(2),
                  kv_spec(KA0, 0), kv_spec(KA0, 1), kv_spec(KA0, 2),
                  kv_spec(VA0, 0), kv_spec(VA0, 1), kv_spec(VA0, 2)],
        out_specs=pl.BlockSpec((ATT_TQ, HEAD_DIM), lambda b, hh, qi: (b * nq + qi, hh)),
        out_shape=jax.ShapeDtypeStruct((t_tok, g4 * HEAD_DIM), BF16),
        compiler_params=pltpu.CompilerParams(
            dimension_semantics=("arbitrary", "arbitrary", "arbitrary"), vmem_limit_bytes=VMEM_LIMIT),
        name="attn_a",
    )(*([qkv] * 9))


MG_TM = 256
MG_CH = 512


def _sigmoid(z):
    return 1.0 / (1.0 + jnp.exp(-z))


def _merge_kernel(x_ref, oa_ref, ob_ref, gmix_ref, wg_ref, wa_ref, wb_ref, wo_ref, o_ref, m_scr):
    x = x_ref[...]
    ms = jnp.mean(x * x, axis=-1, keepdims=True)
    h = (x * lax.rsqrt(ms + EPS) * gmix_ref[...]).astype(BF16)
    oa = oa_ref[...]
    ob = ob_ref[...]
    for c in range(D_MODEL // MG_CH):
        lo, hi = c * MG_CH, (c + 1) * MG_CH
        ga = jnp.dot(h, wg_ref[:, lo:hi], preferred_element_type=F32)
        gb = jnp.dot(h, wg_ref[:, D_MODEL + lo:D_MODEL + hi], preferred_element_type=F32)
        pa = jnp.dot(oa, wa_ref[:, lo:hi], preferred_element_type=F32)
        pb = jnp.dot(ob, wb_ref[:, lo:hi], preferred_element_type=F32)
        m_scr[:, lo:hi] = (_sigmoid(ga) * pa + _sigmoid(gb) * pb).astype(BF16)
    o_ref[...] = x + jnp.dot(m_scr[...], wo_ref[...], preferred_element_type=F32)


def _resident(shape):
    nd = len(shape)
    return pl.BlockSpec(shape, lambda i: (0,) * nd, pipeline_mode=pl.Buffered(1))


def _merge(x2, oa, ob, gmix, w_g, w_a, w_b, w_o):
    t_tok = x2.shape[0]
    return pl.pallas_call(
        _merge_kernel,
        grid=(t_tok // MG_TM,),
        in_specs=[pl.BlockSpec((MG_TM, D_MODEL), lambda i: (i, 0)),
                  pl.BlockSpec((MG_TM, oa.shape[1]), lambda i: (i, 0)),
                  pl.BlockSpec((MG_TM, ob.shape[1]), lambda i: (i, 0)),
                  _resident(gmix.shape), _resident(w_g.shape), _resident(w_a.shape),
                  _resident(w_b.shape), _resident(w_o.shape)],
        out_specs=pl.BlockSpec((MG_TM, D_MODEL), lambda i: (i, 0)),
        out_shape=jax.ShapeDtypeStruct((t_tok, D_MODEL), F32),
        scratch_shapes=[pltpu.VMEM((MG_TM, D_MODEL), BF16)],
        compiler_params=pltpu.CompilerParams(dimension_semantics=("arbitrary",),
                                             vmem_limit_bytes=VMEM_LIMIT),
        name="merge",
    )(x2, oa, ob, gmix, w_g, w_a, w_b, w_o)


RT_TM = 256
CAND_ROWS = PEER_TOPK + (PEER_TOPK - 1) * 8


def _top16_rows(s, key_f, val_ref, idx_ref):
    for r in range(PEER_TOPK):
        m = jnp.max(s, axis=0, keepdims=True)
        idx = jnp.min(jnp.where(s == m, key_f, float(PEER_NKEYS)), axis=0, keepdims=True)
        val_ref[r:r + 1, :] = m
        idx_ref[r:r + 1, :] = idx
        s = jnp.where(key_f == idx, -jnp.inf, s)


def _peer_route_kernel(x_ref, gffn_ref, wpq_ref, sk_ref, e_ref, g_ref,
                       q_scr, v1_scr, i1_scr, v2_scr, i2_scr, cv_scr, ce_scr):
    x = x_ref[...]
    ms = jnp.mean(x * x, axis=-1, keepdims=True)
    h = (x * lax.rsqrt(ms + EPS) * gffn_ref[...]).astype(BF16)
    q = jnp.dot(h, wpq_ref[...], preferred_element_type=F32)
    for hc in range(2 * PEER_HEADS):
        q_scr[hc] = q[:, hc * LANES:(hc + 1) * LANES].astype(BF16)

    key_f = lax.broadcasted_iota(jnp.int32, (PEER_NKEYS, LANES), 0).astype(F32)
    row = lax.broadcasted_iota(jnp.int32, (CAND_ROWS, LANES), 0)
    ca = jnp.where(row < PEER_TOPK, 0, 1 + ((row - PEER_TOPK) >> 3))
    cb = jnp.where(row < PEER_TOPK, row, (row - PEER_TOPK) & 7)
    flat_f = (ca * PEER_TOPK + cb).astype(F32)
    cand_ok = (ca + 1) * (cb + 1) <= PEER_TOPK

    def head_body(hd, carry):
        for lh in range(RT_TM // LANES):
            for c, (val_ref, idx_ref) in enumerate(((v1_scr, i1_scr), (v2_scr, i2_scr))):
                s = lax.dot_general(sk_ref[hd * 2 + c], q_scr[hd * 2 + c, pl.ds(lh * LANES, LANES), :],
                                    NT, preferred_element_type=F32)
                _top16_rows(s, key_f, val_ref, idx_ref)
            v1, i1 = v1_scr[...], i1_scr[...]
            v2, i2 = v2_scr[...], i2_scr[...]
            cand = [v1[0:1] + v2]
            cexp = [i1[0:1] * float(PEER_NKEYS) + i2]
            for a in range(1, PEER_TOPK):
                cand.append(v1[a:a + 1] + v2[0:8])
                cexp.append(i1[a:a + 1] * float(PEER_NKEYS) + i2[0:8])
            cand = jnp.where(cand_ok, jnp.concatenate(cand, axis=0), -jnp.inf)
            cexp = jnp.concatenate(cexp, axis=0)
            for r in range(PEER_TOPK):
                m = jnp.max(cand, axis=0, keepdims=True)
                sel = jnp.min(jnp.where(cand == m, flat_f, 1e9), axis=0, keepdims=True)
                hit = flat_f == sel
                cv_scr[r:r + 1, :] = m
                ce_scr[r:r + 1, :] = jnp.max(jnp.where(hit, cexp, -1.0), axis=0, keepdims=True)
                cand = jnp.where(hit, -jnp.inf, cand)
            cv = cv_scr[...]
            w = jnp.exp(cv - cv[0:1])
            w = w / jnp.sum(w, axis=0, keepdims=True)
            rows = pl.ds(pl.multiple_of(hd * PEER_TOPK, PEER_TOPK), PEER_TOPK)
            e_ref[rows, lh * LANES:(lh + 1) * LANES] = ce_scr[...].astype(jnp.int32)
            g_ref[rows, lh * LANES:(lh + 1) * LANES] = w
        return carry

    lax.fori_loop(0, PEER_HEADS, head_body, 0)


def _peer_route(x1, gffn, w_pq, sk):
    t_tok = x1.shape[0]
    small = [pltpu.VMEM((PEER_TOPK, LANES), F32) for _ in range(6)]
    return pl.pallas_call(
        _peer_route_kernel,
        grid=(t_tok // RT_TM,),
        in_specs=[pl.BlockSpec((RT_TM, D_MODEL), lambda i: (i, 0)),
                  _resident(gffn.shape), _resident(w_pq.shape), _resident(sk.shape)],
        out_specs=[pl.BlockSpec((PEER_SEL, RT_TM), lambda i: (0, i)),
                   pl.BlockSpec((PEER_SEL, RT_TM), lambda i: (0, i))],
        out_shape=[jax.ShapeDtypeStruct((PEER_SEL, t_tok), jnp.int32),
                   jax.ShapeDtypeStruct((PEER_SEL, t_tok), F32)],
        scratch_shapes=[pltpu.VMEM((2 * PEER_HEADS, RT_TM, LANES), BF16)] + small,
        compiler_params=pltpu.CompilerParams(dimension_semantics=("arbitrary",),
                                             vmem_limit_bytes=VMEM_LIMIT),
        name="peer_route",
    )(x1, gffn, w_pq, sk)


MX_TB = 128
MX_UNROLL = 8


def _peer_mix_kernel(e_ref, x_ref, g_ref, gffn_ref, tab_ref, y_ref, buf, sem, h_scr, a_scr, c_scr):
    x = x_ref[...].reshape(MX_TB // 8, ROW_CHUNKS, 8, LANES)
    ss = jnp.sum(jnp.sum(x * x, axis=3, keepdims=True), axis=1, keepdims=True)
    h = x * lax.rsqrt(ss * (1.0 / D_MODEL) + EPS) * gffn_ref[...]
    h_scr[...] = h.reshape(MX_TB * ROW_CHUNKS, LANES)

    def row_copy(t, k, slot):
        return pltpu.make_async_copy(tab_ref.at[e_ref[k, t]], buf.at[slot, k], sem.at[slot])

    def issue(t, slot):
        def chunk(c, carry):
            for kk in range(MX_UNROLL):
                row_copy(t, c * MX_UNROLL + kk, slot).start()
            return carry
        lax.fori_loop(0, PEER_SEL // MX_UNROLL, chunk, 0)

    def wait_all(slot):
        pltpu.make_async_copy(tab_ref.at[pl.ds(0, PEER_SEL)], buf.at[slot], sem.at[slot]).wait()

    issue(0, 0)
    lane = lax.broadcasted_iota(jnp.int32, (PEER_SEL, MX_TB), 1)

    def body(t, carry):
        slot = t & 1

        @pl.when(t + 1 < MX_TB)
        def _():
            issue(t + 1, 1 - slot)

        wait_all(slot)
        rows = _row_slice(t)
        hrow = h_scr[rows, :]

        def u_chunk(c, carry):
            for kk in range(MX_UNROLL):
                k = c * MX_UNROLL + kk
                u = buf[slot, k, 0:ROW_CHUNKS, :].astype(F32)
                a_scr[pl.ds(k, 1), :] = jnp.sum(u * hrow, axis=0, keepdims=True)
            return carry
        lax.fori_loop(0, PEER_SEL // MX_UNROLL, u_chunk, 0)

        a = jnp.sum(a_scr[...], axis=1, keepdims=True)
        gate = jnp.sum(jnp.where(lane == t, g_ref[...], 0.0), axis=1, keepdims=True)
        coef = 0.5 * a * (1.0 + lax.erf(a * INV_SQRT2)) * gate
        c_scr[...] = jnp.broadcast_to(coef, (PEER_SEL, LANES))

        def v_chunk(c, accs):
            accs = list(accs)
            for kk in range(MX_UNROLL):
                k = c * MX_UNROLL + kk
                v = buf[slot, k, ROW_CHUNKS:2 * ROW_CHUNKS, :].astype(F32)
                accs[kk % 4] = accs[kk % 4] + c_scr[pl.ds(k, 1), :] * v
            return tuple(accs)
        zero = jnp.zeros((ROW_CHUNKS, LANES), F32)
        accs = lax.fori_loop(0, PEER_SEL // MX_UNROLL, v_chunk, (zero, zero, zero, zero))
        y_ref[rows, :] = x_ref[rows, :] + ((accs[0] + accs[1]) + (accs[2] + accs[3]))
        return carry

    lax.fori_loop(0, MX_TB, body, 0)


def _peer_mix(x1, e_t, g_t, gffn, table):
    t_tok = x1.shape[0]
    rows = MX_TB * ROW_CHUNKS
    y_rows = pl.pallas_call(
        _peer_mix_kernel,
        grid=(t_tok // MX_TB,),
        in_specs=[pl.BlockSpec((PEER_SEL, MX_TB), lambda i: (0, i), memory_space=pltpu.SMEM),
                  pl.BlockSpec((rows, LANES), lambda i: (i, 0)),
                  pl.BlockSpec((PEER_SEL, MX_TB), lambda i: (0, i)),
                  pl.BlockSpec((ROW_CHUNKS, 1, LANES), lambda i: (0, 0, 0)),
                  pl.BlockSpec(memory_space=pl.ANY)],
        out_specs=pl.BlockSpec((rows, LANES), lambda i: (i, 0)),
        out_shape=jax.ShapeDtypeStruct((t_tok * ROW_CHUNKS, LANES), F32),
        scratch_shapes=[pltpu.VMEM((2, PEER_SEL, 2 * ROW_CHUNKS, LANES), BF16),
                        pltpu.SemaphoreType.DMA((2,)),
                        pltpu.VMEM((rows, LANES), F32),
                        pltpu.VMEM((PEER_SEL, LANES), F32),
                        pltpu.VMEM((PEER_SEL, LANES), F32)],
        compiler_params=pltpu.CompilerParams(dimension_semantics=("arbitrary",),
                                             vmem_limit_bytes=VMEM_LIMIT),
        name="peer_mix",
    )(e_t, _to_rows(x1), g_t, gffn.reshape(ROW_CHUNKS, 1, LANES), table)
    return _from_rows(y_rows, t_tok)


def _prepare(g_mix, w_in, qn_a, kn_a, qn_b, kn_b, w_br_a, w_br_b, w_out, g_ffn, w_pq, sub_keys,
             u_emb, v_emb):
    ones = jnp.ones((HEAD_DIM,), F32)
    gains = jnp.stack([qn_a] * 6 + [kn_a] * 6 + [ones] * 6 + [qn_b] * 4 + [kn_b] + [ones])
    return dict(
        gmix=g_mix.reshape(1, D_MODEL),
        w_qkv=w_in[:, :QKV_WIDTH].astype(BF16),
        w_g=w_in[:, QKV_WIDTH:].astype(BF16),
        gains=gains.reshape(IN_NJ, 1, HEAD_DIM),
        tables=_rope_tables(),
        w_a=w_br_a.astype(BF16), w_b=w_br_b.astype(BF16), w_o=w_out.astype(BF16),
        gffn=g_ffn.reshape(1, D_MODEL),
        w_pq=w_pq.astype(BF16),
        sk=sub_keys.reshape(2 * PEER_HEADS, PEER_NKEYS, LANES).astype(BF16),
        table=_pack_table(u_emb, v_emb),
    )


def _layer(x, p):
    n_batch, length, d = x.shape
    assert length == SEQ and d == D_MODEL
    x2 = x.reshape(n_batch * length, d)
    qkv = _in_proj(x2, p["gmix"], p["w_qkv"], p["gains"], p["tables"])
    ob = _attn_b(qkv, n_batch)
    oa = _attn_a(qkv, n_batch)
    x1 = _merge(x2, oa, ob, p["gmix"], p["w_g"], p["w_a"], p["w_b"], p["w_o"])
    e_t, g_t = _peer_route(x1, p["gffn"], p["w_pq"], p["sk"])
    y = _peer_mix(x1, e_t, g_t, p["gffn"], p["table"])
    return y.reshape(n_batch, length, d)


def kernel(x_prompt, x_sample, g_mix, w_in, qn_a, kn_a, qn_b, kn_b, w_br_a, w_br_b, w_out, g_ffn,
           w_pq, sub_keys, u_emb, v_emb):
    y_prompt, y_sample = x_prompt, x_sample
    for l in range(g_mix.shape[0]):
        p = _prepare(g_mix[l], w_in[l], qn_a[l], kn_a[l], qn_b[l], kn_b[l], w_br_a[l], w_br_b[l],
                     w_out[l], g_ffn[l], w_pq[l], sub_keys[l], u_emb[l], v_emb[l])
        y_prompt = _layer(y_prompt, p)
        y_sample = _layer(y_sample, p)
    return (y_prompt, y_sample)
```

```python
import functools
import math

import jax
import jax.numpy as jnp
from jax import lax
from jax.experimental import pallas as pl
from jax.experimental.pallas import tpu as pltpu

F32 = jnp.float32
BF16 = jnp.bfloat16

D_MODEL = 2048
SEQ = 2048
HEAD_DIM = 128
EPS = 1e-6
GRID_W = 64
SCALE = HEAD_DIM ** -0.5
A_HEADS_PER_GROUP = 4
A_DILATIONS = (1, 4, 16)
A_HALF = 64
ROPE_THETA_A = 500000.0
ROT_A = HEAD_DIM // 4
ROPE_THETA_B = 10000.0
AXIAL_HALF = HEAD_DIM // 2
QKV_WIDTH = 6144
QA0, KA0, VA0, QB0, KB0, VB0 = 0, 12, 24, 36, 44, 46
N_QKV_HEADS = 48
PEER_HEADS = 8
PEER_NKEYS = 128
PEER_TOPK = 16
PEER_SEL = PEER_HEADS * PEER_TOPK
ROW_CHUNKS = D_MODEL // 128
NEG = -1e30
INV_SQRT2 = 0.7071067811865476

LANES = 128
VMEM_LIMIT = 56 * 1024 * 1024


def _to_rows(x2):
    n = x2.shape[0]
    return x2.reshape(n // 8, 8, ROW_CHUNKS, LANES).transpose(0, 2, 1, 3).reshape(n * ROW_CHUNKS, LANES)


def _from_rows(r2, n):
    return r2.reshape(n // 8, ROW_CHUNKS, 8, LANES).transpose(0, 2, 1, 3).reshape(n, ROW_CHUNKS * LANES)


def _row_slice(i):
    return pl.ds((i >> 3) * (8 * ROW_CHUNKS) + (i & 7), ROW_CHUNKS, stride=8)


PACK_BLOCKS = 8


def _pack_table_kernel(u_ref, v_ref, o_ref):
    for e in range(8 * PACK_BLOCKS):
        o_ref[e, 0:ROW_CHUNKS, :] = u_ref[_row_slice(e), :].astype(BF16)
        o_ref[e, ROW_CHUNKS:2 * ROW_CHUNKS, :] = v_ref[_row_slice(e), :].astype(BF16)


def _pack_table(u_emb, v_emb):
    n = u_emb.shape[0]
    experts = 8 * PACK_BLOCKS
    rows = experts * ROW_CHUNKS
    return pl.pallas_call(
        _pack_table_kernel,
        grid=(n // experts,),
        in_specs=[pl.BlockSpec((rows, LANES), lambda i: (i, 0)),
                  pl.BlockSpec((rows, LANES), lambda i: (i, 0))],
        out_specs=pl.BlockSpec((experts, 2 * ROW_CHUNKS, LANES), lambda i: (i, 0, 0)),
        out_shape=jax.ShapeDtypeStruct((n, 2 * ROW_CHUNKS, LANES), BF16),
        name="pack_table",
    )(_to_rows(u_emb), _to_rows(v_emb))


def _rope_tables():
    pos = jnp.arange(SEQ, dtype=F32)
    ha = ROT_A // 2
    inv_a = ROPE_THETA_A ** (-(jnp.arange(0, ROT_A, 2, dtype=F32) / ROT_A))
    ang = pos[:, None] * inv_a[None, :]
    cos, sin = jnp.cos(ang), jnp.sin(ang)
    pad = jnp.zeros((SEQ, HEAD_DIM - ROT_A), F32)
    zh = jnp.zeros((SEQ, ha), F32)
    ca = jnp.concatenate([cos, cos, pad + 1.0], axis=1)
    s1a = jnp.concatenate([-sin, zh, pad], axis=1)
    s2a = jnp.concatenate([zh, sin, pad], axis=1)
    rows = SEQ // GRID_W
    row_ids = jnp.repeat(jnp.arange(rows), GRID_W).astype(F32)
    col_ids = jnp.tile(jnp.arange(GRID_W), rows).astype(F32)
    inv_b = ROPE_THETA_B ** (-(jnp.arange(0, AXIAL_HALF, 2, dtype=F32) / AXIAL_HALF))
    ar = row_ids[:, None] * inv_b[None, :]
    ac = col_ids[:, None] * inv_b[None, :]
    zq = jnp.zeros_like(ar)
    cb = jnp.concatenate([jnp.cos(ar), jnp.cos(ar), jnp.cos(ac), jnp.cos(ac)], axis=1)
    s1b = jnp.concatenate([-jnp.sin(ar), zq, -jnp.sin(ac), zq], axis=1)
    s2b = jnp.concatenate([zq, jnp.sin(ar), zq, jnp.sin(ac)], axis=1)
    return ca, s1a, s2a, cb, s1b, s2b


IN_TM = 512
IN_TN = 256
IN_NJ = QKV_WIDTH // IN_TN


def _in_proj_kernel(x_ref, gmix_ref, w_ref, gain_ref, ca_ref, s1a_ref, s2a_ref,
                    cb_ref, s1b_ref, s2b_ref, o_ref, h_scr):
    j = pl.program_id(1)

    @pl.when(j == 0)
    def _():
        x = x_ref[...]
        ms = jnp.mean(x * x, axis=-1, keepdims=True)
        h_scr[...] = (x * lax.rsqrt(ms + EPS) * gmix_ref[...]).astype(BF16)

    t = jnp.dot(h_scr[...], w_ref[...], preferred_element_type=F32)
    is_a = j < 12
    is_b = jnp.logical_and(j >= 18, j < 23)

    def norm_rope(c_ref, s1_ref, s2_ref, sh):
        for hh in range(IN_TN // HEAD_DIM):
            y = t[:, hh * HEAD_DIM:(hh + 1) * HEAD_DIM]
            y = y * lax.rsqrt(jnp.mean(y * y, axis=-1, keepdims=True) + EPS) * gain_ref[0]
            out = (y * c_ref[...]
                   + pltpu.roll(y, HEAD_DIM - sh, 1) * s1_ref[...]
                   + pltpu.roll(y, sh, 1) * s2_ref[...])
            o_ref[hh] = out.astype(BF16)

    @pl.when(is_a)
    def _():
        norm_rope(ca_ref, s1a_ref, s2a_ref, ROT_A // 2)

    @pl.when(is_b)
    def _():
        norm_rope(cb_ref, s1b_ref, s2b_ref, AXIAL_HALF // 2)

    @pl.when(jnp.logical_not(jnp.logical_or(is_a, is_b)))
    def _():
        for hh in range(IN_TN // HEAD_DIM):
            o_ref[hh] = t[:, hh * HEAD_DIM:(hh + 1) * HEAD_DIM].astype(BF16)


def _in_proj(x2, gmix, w_qkv, gains, tables):
    t_tok = x2.shape[0]
    nseq = SEQ // IN_TM
    tab_spec = pl.BlockSpec((IN_TM, HEAD_DIM), lambda i, j: (i % nseq, 0))
    return pl.pallas_call(
        _in_proj_kernel,
        grid=(t_tok // IN_TM, IN_NJ),
        in_specs=[pl.BlockSpec((IN_TM, D_MODEL), lambda i, j: (i, 0)),
                  pl.BlockSpec((1, D_MODEL), lambda i, j: (0, 0)),
                  pl.BlockSpec((D_MODEL, IN_TN), lambda i, j: (0, j)),
                  pl.BlockSpec((1, 1, HEAD_DIM), lambda i, j: (j, 0, 0)),
                  tab_spec, tab_spec, tab_spec, tab_spec, tab_spec, tab_spec],
        out_specs=pl.BlockSpec((IN_TN // HEAD_DIM, IN_TM, HEAD_DIM), lambda i, j: (j, i, 0)),
        out_shape=jax.ShapeDtypeStruct((N_QKV_HEADS, t_tok, HEAD_DIM), BF16),
        scratch_shapes=[pltpu.VMEM((IN_TM, D_MODEL), BF16)],
        compiler_params=pltpu.CompilerParams(dimension_semantics=("arbitrary", "arbitrary"),
                                             vmem_limit_bytes=VMEM_LIMIT),
        name="in_proj",
    )(x2, gmix, w_qkv, gains, *tables)


ATT_TQ = 256
NT = (((1,), (1,)), ((), ()))


def _attn_b_kernel(q_ref, k_ref, v_ref, o_ref):
    k = k_ref[0]
    v = v_ref[0]
    for hh in range(4):
        s = lax.dot_general(q_ref[hh], k, NT, preferred_element_type=F32) * SCALE
        m = jnp.max(s, axis=-1, keepdims=True)
        p = jnp.exp(s - m)
        l = jnp.sum(p, axis=-1, keepdims=True)
        o = jnp.dot(p.astype(BF16), v, preferred_element_type=F32) / l
        o_ref[:, hh * HEAD_DIM:(hh + 1) * HEAD_DIM] = o.astype(BF16)


def _attn_b(qkv, n_batch):
    t_tok = qkv.shape[1]
    nq = SEQ // ATT_TQ
    return pl.pallas_call(
        _attn_b_kernel,
        grid=(n_batch, 2, nq),
        in_specs=[pl.BlockSpec((4, ATT_TQ, HEAD_DIM), lambda b, g, qi: (QB0 // 4 + g, b * nq + qi, 0)),
                  pl.BlockSpec((1, SEQ, HEAD_DIM), lambda b, g, qi: (KB0 + g, b, 0)),
                  pl.BlockSpec((1, SEQ, HEAD_DIM), lambda b, g, qi: (VB0 + g, b, 0))],
        out_specs=pl.BlockSpec((ATT_TQ, 4 * HEAD_DIM), lambda b, g, qi: (b * nq + qi, g)),
        out_shape=jax.ShapeDtypeStruct((t_tok, 8 * HEAD_DIM), BF16),
        compiler_params=pltpu.CompilerParams(
            dimension_semantics=("arbitrary", "arbitrary", "arbitrary"), vmem_limit_bytes=VMEM_LIMIT),
        name="attn_b",
    )(qkv, qkv, qkv)


def _key_window(dil):
    span = A_HALF * dil
    span = -(-span // LANES) * LANES
    return min(SEQ, ATT_TQ + 2 * span), span


def _attn_a_kernel(q0_ref, q1_ref, q2_ref, k0_ref, k1_ref, k2_ref, v0_ref, v1_ref, v2_ref, o_ref):
    q0pos = pl.program_id(2) * ATT_TQ
    scores, values = [], []
    for q_ref, k_ref, v_ref, dil in ((q0_ref, k0_ref, v0_ref, A_DILATIONS[0]),
                                     (q1_ref, k1_ref, v1_ref, A_DILATIONS[1]),
                                     (q2_ref, k2_ref, v2_ref, A_DILATIONS[2])):
        width, span = _key_window(dil)
        if width < SEQ:
            start = pl.multiple_of(jnp.clip(q0pos - span, 0, SEQ - width), LANES)
            kk = k_ref[0, pl.ds(start, width), :]
            vv = v_ref[0, pl.ds(start, width), :]
        else:
            start = 0
            kk = k_ref[0]
            vv = v_ref[0]
        s = lax.dot_general(q_ref[0], kk, NT, preferred_element_type=F32) * SCALE
        rel = (lax.broadcasted_iota(jnp.int32, (ATT_TQ, width), 1)
               - lax.broadcasted_iota(jnp.int32, (ATT_TQ, width), 0)) + (start - q0pos)
        valid = jnp.abs(rel) <= A_HALF * dil
        if dil > 1:
            valid = jnp.logical_and(valid, (rel & (dil - 1)) == 0)
        scores.append(jnp.where(valid, s, NEG))
        values.append(vv)
    m = functools.reduce(jnp.maximum, [jnp.max(s, axis=-1, keepdims=True) for s in scores])
    l = jnp.zeros((ATT_TQ, 1), F32)
    acc = jnp.zeros((ATT_TQ, HEAD_DIM), F32)
    for s, vv in zip(scores, values):
        p = jnp.exp(s - m)
        l = l + jnp.sum(p, axis=-1, keepdims=True)
        acc = acc + jnp.dot(p.astype(BF16), vv, preferred_element_type=F32)
    o_ref[...] = (acc / l).astype(BF16)


def _attn_a(qkv, n_batch):
    t_tok = qkv.shape[1]
    nq = SEQ // ATT_TQ
    g4 = A_HEADS_PER_GROUP

    def q_spec(g):
        return pl.BlockSpec((1, ATT_TQ, HEAD_DIM), lambda b, hh, qi: (QA0 + g4 * g + hh, b * nq + qi, 0))

    def kv_spec(base, g):
        return pl.BlockSpec((1, SEQ, HEAD_DIM), lambda b, hh, qi: (base + g4 * g + hh, b, 0))

    return pl.pallas_call(
        _attn_a_kernel,
        grid=(n_batch, g4, nq),
        in_specs=[q_spec(0), q_spec(1), q_spec(2),
                  kv_spec(KA0, 0), kv_spec(KA0, 1), kv_spec(KA0, 2),
                  kv_spec(VA0, 0), kv_spec(VA0, 1), kv_spec(VA0, 2)],
        out_specs=pl.BlockSpec((ATT_TQ, HEAD_DIM), lambda b, hh, qi: (b * nq + qi, hh)),
        out_shape=jax.ShapeDtypeStruct((t_tok, g4 * HEAD_DIM), BF16),
        compiler_params=pltpu.CompilerParams(
            dimension_semantics=("arbitrary", "arbitrary", "arbitrary"), vmem_limit_bytes=VMEM_LIMIT),
        name="attn_a",
    )(*([qkv] * 9))


MG_TM = 256
MG_CH = 512


def _sigmoid(z):
    return 1.0 / (1.0 + jnp.exp(-z))


def _merge_kernel(x_ref, oa_ref, ob_ref, gmix_ref, wg_ref, wa_ref, wb_ref, wo_ref, o_ref, m_scr):
    x = x_ref[...]
    ms = jnp.mean(x * x, axis=-1, keepdims=True)
    h = (x * lax.rsqrt(ms + EPS) * gmix_ref[...]).astype(BF16)
    oa = oa_ref[...]
    ob = ob_ref[...]
    for c in range(D_MODEL // MG_CH):
        lo, hi = c * MG_CH, (c + 1) * MG_CH
        ga = jnp.dot(h, wg_ref[:, lo:hi], preferred_element_type=F32)
        gb = jnp.dot(h, wg_ref[:, D_MODEL + lo:D_MODEL + hi], preferred_element_type=F32)
        pa = jnp.dot(oa, wa_ref[:, lo:hi], preferred_element_type=F32)
        pb = jnp.dot(ob, wb_ref[:, lo:hi], preferred_element_type=F32)
        m_scr[:, lo:hi] = (_sigmoid(ga) * pa + _sigmoid(gb) * pb).astype(BF16)
    o_ref[...] = x + jnp.dot(m_scr[...], wo_ref[...], preferred_element_type=F32)


def _resident(shape):
    nd = len(shape)
    return pl.BlockSpec(shape, lambda i: (0,) * nd, pipeline_mode=pl.Buffered(1))


def _merge(x2, oa, ob, gmix, w_g, w_a, w_b, w_o):
    t_tok = x2.shape[0]
    return pl.pallas_call(
        _merge_kernel,
        grid=(t_tok // MG_TM,),
        in_specs=[pl.BlockSpec((MG_TM, D_MODEL), lambda i: (i, 0)),
                  pl.BlockSpec((MG_TM, oa.shape[1]), lambda i: (i, 0)),
                  pl.BlockSpec((MG_TM, ob.shape[1]), lambda i: (i, 0)),
                  _resident(gmix.shape), _resident(w_g.shape), _resident(w_a.shape),
                  _resident(w_b.shape), _resident(w_o.shape)],
        out_specs=pl.BlockSpec((MG_TM, D_MODEL), lambda i: (i, 0)),
        out_shape=jax.ShapeDtypeStruct((t_tok, D_MODEL), F32),
        scratch_shapes=[pltpu.VMEM((MG_TM, D_MODEL), BF16)],
        compiler_params=pltpu.CompilerParams(dimension_semantics=("arbitrary",),
                                             vmem_limit_bytes=VMEM_LIMIT),
        name="merge",
    )(x2, oa, ob, gmix, w_g, w_a, w_b, w_o)


RT_TM = 256
CAND_ROWS = PEER_TOPK + (PEER_TOPK - 1) * 8


def _top16_rows(s, key_f, val_ref, idx_ref):
    for r in range(PEER_TOPK):
        m = jnp.max(s, axis=0, keepdims=True)
        idx = jnp.min(jnp.where(s == m, key_f, float(PEER_NKEYS)), axis=0, keepdims=True)
        val_ref[r:r + 1, :] = m
        idx_ref[r:r + 1, :] = idx
        s = jnp.where(key_f == idx, -jnp.inf, s)


PERM8 = (0, 4, 2, 6, 1, 5, 3, 7)


def _coef_row(q):
    return (q // 8) * 8 + PERM8[q % 8]


def _peer_route_kernel(x_ref, gffn_ref, wpq_ref, sk_ref, e_ref, g_ref,
                       q_scr, v1_scr, i1_scr, v2_scr, i2_scr, cv_scr, ce_scr, et_scr):
    x = x_ref[...]
    ms = jnp.mean(x * x, axis=-1, keepdims=True)
    h = (x * lax.rsqrt(ms + EPS) * gffn_ref[...]).astype(BF16)
    q = jnp.dot(h, wpq_ref[...], preferred_element_type=F32)
    for hc in range(2 * PEER_HEADS):
        q_scr[hc] = q[:, hc * LANES:(hc + 1) * LANES].astype(BF16)

    key_f = lax.broadcasted_iota(jnp.int32, (PEER_NKEYS, LANES), 0).astype(F32)
    row = lax.broadcasted_iota(jnp.int32, (CAND_ROWS, LANES), 0)
    ca = jnp.where(row < PEER_TOPK, 0, 1 + ((row - PEER_TOPK) >> 3))
    cb = jnp.where(row < PEER_TOPK, row, (row - PEER_TOPK) & 7)
    flat_f = (ca * PEER_TOPK + cb).astype(F32)
    cand_ok = (ca + 1) * (cb + 1) <= PEER_TOPK

    def head_body(hd, carry):
        for lh in range(RT_TM // LANES):
            for c, (val_ref, idx_ref) in enumerate(((v1_scr, i1_scr), (v2_scr, i2_scr))):
                s = lax.dot_general(sk_ref[hd * 2 + c], q_scr[hd * 2 + c, pl.ds(lh * LANES, LANES), :],
                                    NT, preferred_element_type=F32)
                _top16_rows(s, key_f, val_ref, idx_ref)
            v1, i1 = v1_scr[...], i1_scr[...]
            v2, i2 = v2_scr[...], i2_scr[...]
            cand = [v1[0:1] + v2]
            cexp = [i1[0:1] * float(PEER_NKEYS) + i2]
            for a in range(1, PEER_TOPK):
                cand.append(v1[a:a + 1] + v2[0:8])
                cexp.append(i1[a:a + 1] * float(PEER_NKEYS) + i2[0:8])
            cand = jnp.where(cand_ok, jnp.concatenate(cand, axis=0), -jnp.inf)
            cexp = jnp.concatenate(cexp, axis=0)
            for r in range(PEER_TOPK):
                m = jnp.max(cand, axis=0, keepdims=True)
                sel = jnp.min(jnp.where(cand == m, flat_f, 1e9), axis=0, keepdims=True)
                hit = flat_f == sel
                cv_scr[_coef_row(r):_coef_row(r) + 1, :] = m
                ce_scr[r:r + 1, :] = jnp.max(jnp.where(hit, cexp, -1.0), axis=0, keepdims=True)
                cand = jnp.where(hit, -jnp.inf, cand)
            cv = cv_scr[...]
            w = jnp.exp(cv - cv[0:1])
            w = w / jnp.sum(w, axis=0, keepdims=True)
            rows = pl.ds(pl.multiple_of(hd * PEER_TOPK, PEER_TOPK), PEER_TOPK)
            et_scr[rows, lh * LANES:(lh + 1) * LANES] = ce_scr[...]
            g_ref[rows, lh * LANES:(lh + 1) * LANES] = w
        return carry

    lax.fori_loop(0, PEER_HEADS, head_body, 0)
    e_ref[...] = et_scr[...].T.astype(jnp.int32)


def _peer_route(x1, gffn, w_pq, sk):
    t_tok = x1.shape[0]
    small = [pltpu.VMEM((PEER_TOPK, LANES), F32) for _ in range(6)]
    return pl.pallas_call(
        _peer_route_kernel,
        grid=(t_tok // RT_TM,),
        in_specs=[pl.BlockSpec((RT_TM, D_MODEL), lambda i: (i, 0)),
                  _resident(gffn.shape), _resident(w_pq.shape), _resident(sk.shape)],
        out_specs=[pl.BlockSpec((RT_TM, PEER_SEL), lambda i: (i, 0)),
                   pl.BlockSpec((PEER_SEL, RT_TM), lambda i: (0, i))],
        out_shape=[jax.ShapeDtypeStruct((t_tok, PEER_SEL), jnp.int32),
                   jax.ShapeDtypeStruct((PEER_SEL, t_tok), F32)],
        scratch_shapes=[pltpu.VMEM((2 * PEER_HEADS, RT_TM, LANES), BF16)] + small
                       + [pltpu.VMEM((PEER_SEL, RT_TM), F32)],
        compiler_params=pltpu.CompilerParams(dimension_semantics=("arbitrary",),
                                             vmem_limit_bytes=VMEM_LIMIT),
        name="peer_route",
    )(x1, gffn, w_pq, sk)


MX_TB = 128
MX_SLOTS = 8
MX_AHEAD = 5


def _merge8(ps, sub):
    lo4, lo2, lo1 = (sub & 4) == 0, (sub & 2) == 0, (sub & 1) == 0
    q = [jnp.where(lo4, ps[2 * i], ps[2 * i + 1])
         + pltpu.roll(jnp.where(lo4, ps[2 * i + 1], ps[2 * i]), 4, 0) for i in range(4)]
    r = [jnp.where(lo2, q[2 * i] + pltpu.roll(q[2 * i], 6, 0), q[2 * i + 1] + pltpu.roll(q[2 * i + 1], 2, 0))
         for i in range(2)]
    return jnp.where(lo1, r[0] + pltpu.roll(r[0], 7, 0), r[1] + pltpu.roll(r[1], 1, 0))


def _peer_mix_kernel(e_ref, en_ref, x_ref, g_ref, gffn_ref, tab_ref, y_ref, buf, sem, h_scr, m_scr,
                     c_scr):
    step_i = pl.program_id(0)
    x = x_ref[...].reshape(MX_TB // 8, ROW_CHUNKS, 8, LANES)
    ss = jnp.sum(jnp.sum(x * x, axis=3, keepdims=True), axis=1, keepdims=True)
    h = x * lax.rsqrt(ss * (1.0 / D_MODEL) + EPS) * gffn_ref[...]
    h_scr[...] = h.reshape(MX_TB * ROW_CHUNKS, LANES)

    sub = lax.broadcasted_iota(jnp.int32, (8, LANES), 0)
    lane = lax.broadcasted_iota(jnp.int32, (PEER_SEL, MX_TB), 1)

    def row_copy(idx_ref, tok, k, slot):
        return pltpu.make_async_copy(tab_ref.at[idx_ref[tok, k]], buf.at[slot, k], sem.at[slot])

    def wait_rows(slot):
        pltpu.make_async_copy(tab_ref.at[pl.ds(0, PEER_SEL)], buf.at[slot], sem.at[slot]).wait()

    def step(s, r, idx_ref, itok, do_u, do_c, do_v):
        slot_v, slot_u, slot_i = r % MX_SLOTS, (r + 2) % MX_SLOTS, (r + MX_AHEAD) % MX_SLOTS
        par = r % 2
        if do_u:
            wait_rows(slot_u)
            hrow = h_scr[_row_slice(s + 2), :]
            hlo, hhi = hrow[0:8], hrow[8:16]
        if do_c:
            a = jnp.sum(m_scr[1 - par], axis=1, keepdims=True)
            gate = jnp.sum(jnp.where(lane == s + 1, g_ref[...], 0.0), axis=1, keepdims=True)
            coef = 0.5 * a * (1.0 + lax.erf(a * INV_SQRT2)) * gate
        zero = jnp.zeros((ROW_CHUNKS, LANES), F32)
        accs = [zero, zero, zero, zero]
        merged = []
        for j in range(PEER_SEL // 8):
            parts = []
            for kk in range(8):
                q = 8 * j + kk
                if idx_ref is not None:
                    row_copy(idx_ref, itok, q, slot_i).start(priority=q % 2)
                if do_u:
                    u = buf[slot_u, q, 0:ROW_CHUNKS, :].astype(F32)
                    parts.append(u[0:8] * hlo + u[8:16] * hhi)
                if do_v:
                    v = buf[slot_v, q, ROW_CHUNKS:2 * ROW_CHUNKS, :].astype(F32)
                    accs[kk % 4] = accs[kk % 4] + c_scr[par, pl.ds(_coef_row(q), 1), :] * v
            if do_u:
                merged.append(_merge8(parts, sub))
        if do_v:
            rows = _row_slice(s)
            y_ref[rows, :] = x_ref[rows, :] + ((accs[0] + accs[1]) + (accs[2] + accs[3]))
        if do_u:
            for j in range(PEER_SEL // 8):
                m_scr[par, 8 * j:8 * j + 8, :] = merged[j]
        if do_c:
            c_scr[1 - par] = jnp.broadcast_to(coef, (PEER_SEL, LANES))

    @pl.when(step_i == 0)
    def _():
        for tok in range(MX_AHEAD):
            for q in range(PEER_SEL):
                row_copy(e_ref, tok, q, tok).start(priority=q % 2)

    step(-2, MX_SLOTS - 2, None, None, True, False, False)
    step(-1, MX_SLOTS - 1, None, None, True, True, False)

    def body(it, carry):
        for r in range(MX_SLOTS):
            s = it * MX_SLOTS + r
            step(s, r, e_ref, s + MX_AHEAD, True, True, True)
        return carry

    lax.fori_loop(0, MX_TB // MX_SLOTS - 1, body, 0)
    for s in range(MX_TB - MX_SLOTS, MX_TB):
        nxt = s + MX_AHEAD - MX_TB
        src, tok = (e_ref, s + MX_AHEAD) if nxt < 0 else (en_ref, nxt)
        step(s, s % MX_SLOTS, src, tok, s + 2 < MX_TB, s + 1 < MX_TB, True)

    @pl.when(step_i == pl.num_programs(0) - 1)
    def _():
        for tok in range(MX_AHEAD):
            wait_rows(tok)


def _peer_mix(x1, e_t, g_t, gffn, table):
    t_tok = x1.shape[0]
    assert MX_SLOTS & (MX_SLOTS - 1) == 0 and MX_TB % MX_SLOTS == 0 and MX_AHEAD <= 8
    rows = MX_TB * ROW_CHUNKS
    last8 = t_tok // 8 - 1
    y_rows = pl.pallas_call(
        _peer_mix_kernel,
        grid=(t_tok // MX_TB,),
        in_specs=[pl.BlockSpec((MX_TB, PEER_SEL), lambda i: (i, 0), memory_space=pltpu.SMEM),
                  pl.BlockSpec((8, PEER_SEL), lambda i: (jnp.minimum((i + 1) * (MX_TB // 8), last8), 0),
                               memory_space=pltpu.SMEM),
                  pl.BlockSpec((rows, LANES), lambda i: (i, 0)),
                  pl.BlockSpec((PEER_SEL, MX_TB), lambda i: (0, i)),
                  pl.BlockSpec((ROW_CHUNKS, 1, LANES), lambda i: (0, 0, 0)),
                  pl.BlockSpec(memory_space=pl.ANY)],
        out_specs=pl.BlockSpec((rows, LANES), lambda i: (i, 0)),
        out_shape=jax.ShapeDtypeStruct((t_tok * ROW_CHUNKS, LANES), F32),
        scratch_shapes=[pltpu.VMEM((MX_SLOTS, PEER_SEL, 2 * ROW_CHUNKS, LANES), BF16),
                        pltpu.SemaphoreType.DMA((MX_SLOTS,)),
                        pltpu.VMEM((rows, LANES), F32),
                        pltpu.VMEM((2, PEER_SEL, LANES), F32),
                        pltpu.VMEM((2, PEER_SEL, LANES), F32)],
        compiler_params=pltpu.CompilerParams(dimension_semantics=("arbitrary",),
                                             vmem_limit_bytes=VMEM_LIMIT),
        name="peer_mix",
    )(e_t, e_t, _to_rows(x1), g_t, gffn.reshape(ROW_CHUNKS, 1, LANES), table)
    return _from_rows(y_rows, t_tok)


def _prepare(g_mix, w_in, qn_a, kn_a, qn_b, kn_b, w_br_a, w_br_b, w_out, g_ffn, w_pq, sub_keys,
             u_emb, v_emb):
    ones = jnp.ones((HEAD_DIM,), F32)
    gains = jnp.stack([qn_a] * 6 + [kn_a] * 6 + [ones] * 6 + [qn_b] * 4 + [kn_b] + [ones])
    return dict(
        gmix=g_mix.reshape(1, D_MODEL),
        w_qkv=w_in[:, :QKV_WIDTH].astype(BF16),
        w_g=w_in[:, QKV_WIDTH:].astype(BF16),
        gains=gains.reshape(IN_NJ, 1, HEAD_DIM),
        tables=_rope_tables(),
        w_a=w_br_a.astype(BF16), w_b=w_br_b.astype(BF16), w_o=w_out.astype(BF16),
        gffn=g_ffn.reshape(1, D_MODEL),
        w_pq=w_pq.astype(BF16),
        sk=sub_keys.reshape(2 * PEER_HEADS, PEER_NKEYS, LANES).astype(BF16),
        table=_pack_table(u_emb, v_emb),
    )


def _layer(x, p):
    n_batch, length, d = x.shape
    assert length == SEQ and d == D_MODEL
    x2 = x.reshape(n_batch * length, d)
    qkv = _in_proj(x2, p["gmix"], p["w_qkv"], p["gains"], p["tables"])
    ob = _attn_b(qkv, n_batch)
    oa = _attn_a(qkv, n_batch)
    x1 = _merge(x2, oa, ob, p["gmix"], p["w_g"], p["w_a"], p["w_b"], p["w_o"])
    e_t, g_t = _peer_route(x1, p["gffn"], p["w_pq"], p["sk"])
    y = _peer_mix(x1, e_t, g_t, p["gffn"], p["table"])
    return y.reshape(n_batch, length, d)


def kernel(x_prompt, x_sample, g_mix, w_in, qn_a, kn_a, qn_b, kn_b, w_br_a, w_br_b, w_out, g_ffn,
           w_pq, sub_keys, u_emb, v_emb):
    y_prompt, y_sample = x_prompt, x_sample
    for l in range(g_mix.shape[0]):
        p = _prepare(g_mix[l], w_in[l], qn_a[l], kn_a[l], qn_b[l], kn_b[l], w_br_a[l], w_br_b[l],
                     w_out[l], g_ffn[l], w_pq[l], sub_keys[l], u_emb[l], v_emb[l])
        y_prompt = _layer(y_prompt, p)
        y_sample = _layer(y_sample, p)
    return (y_prompt, y_sample)
```

```python
import functools
import math

import jax
import jax.numpy as jnp
from jax import lax
from jax.experimental import pallas as pl
from jax.experimental.pallas import tpu as pltpu

F32 = jnp.float32
BF16 = jnp.bfloat16

D_MODEL = 2048
SEQ = 2048
HEAD_DIM = 128
EPS = 1e-6
GRID_W = 64
SCALE = HEAD_DIM ** -0.5
A_HEADS_PER_GROUP = 4
A_DILATIONS = (1, 4, 16)
A_HALF = 64
ROPE_THETA_A = 500000.0
ROT_A = HEAD_DIM // 4
ROPE_THETA_B = 10000.0
AXIAL_HALF = HEAD_DIM // 2
QKV_WIDTH = 6144
QA0, KA0, VA0, QB0, KB0, VB0 = 0, 12, 24, 36, 44, 46
N_QKV_HEADS = 48
PEER_HEADS = 8
PEER_NKEYS = 128
PEER_TOPK = 16
PEER_SEL = PEER_HEADS * PEER_TOPK
ROW_CHUNKS = D_MODEL // 128
NEG = -1e30
INV_SQRT2 = 0.7071067811865476

LANES = 128
VMEM_LIMIT = 56 * 1024 * 1024


def _to_rows(x2):
    n = x2.shape[0]
    return x2.reshape(n // 8, 8, ROW_CHUNKS, LANES).transpose(0, 2, 1, 3).reshape(n * ROW_CHUNKS, LANES)


def _from_rows(r2, n):
    return r2.reshape(n // 8, ROW_CHUNKS, 8, LANES).transpose(0, 2, 1, 3).reshape(n, ROW_CHUNKS * LANES)


def _row_slice(i):
    return pl.ds((i >> 3) * (8 * ROW_CHUNKS) + (i & 7), ROW_CHUNKS, stride=8)


PACK_BLOCKS = 8


def _pack_table_kernel(u_ref, v_ref, o_ref):
    for e in range(8 * PACK_BLOCKS):
        o_ref[e, 0:ROW_CHUNKS, :] = u_ref[_row_slice(e), :].astype(BF16)
        o_ref[e, ROW_CHUNKS:2 * ROW_CHUNKS, :] = v_ref[_row_slice(e), :].astype(BF16)


def _pack_table(u_emb, v_emb):
    n = u_emb.shape[0]
    experts = 8 * PACK_BLOCKS
    rows = experts * ROW_CHUNKS
    return pl.pallas_call(
        _pack_table_kernel,
        grid=(n // experts,),
        in_specs=[pl.BlockSpec((rows, LANES), lambda i: (i, 0)),
                  pl.BlockSpec((rows, LANES), lambda i: (i, 0))],
        out_specs=pl.BlockSpec((experts, 2 * ROW_CHUNKS, LANES), lambda i: (i, 0, 0)),
        out_shape=jax.ShapeDtypeStruct((n, 2 * ROW_CHUNKS, LANES), BF16),
        name="pack_table",
    )(_to_rows(u_emb), _to_rows(v_emb))


def _rope_tables():
    pos = jnp.arange(SEQ, dtype=F32)
    ha = ROT_A // 2
    inv_a = ROPE_THETA_A ** (-(jnp.arange(0, ROT_A, 2, dtype=F32) / ROT_A))
    ang = pos[:, None] * inv_a[None, :]
    cos, sin = jnp.cos(ang), jnp.sin(ang)
    pad = jnp.zeros((SEQ, HEAD_DIM - ROT_A), F32)
    zh = jnp.zeros((SEQ, ha), F32)
    ca = jnp.concatenate([cos, cos, pad + 1.0], axis=1)
    s1a = jnp.concatenate([-sin, zh, pad], axis=1)
    s2a = jnp.concatenate([zh, sin, pad], axis=1)
    rows = SEQ // GRID_W
    row_ids = jnp.repeat(jnp.arange(rows), GRID_W).astype(F32)
    col_ids = jnp.tile(jnp.arange(GRID_W), rows).astype(F32)
    inv_b = ROPE_THETA_B ** (-(jnp.arange(0, AXIAL_HALF, 2, dtype=F32) / AXIAL_HALF))
    ar = row_ids[:, None] * inv_b[None, :]
    ac = col_ids[:, None] * inv_b[None, :]
    zq = jnp.zeros_like(ar)
    cb = jnp.concatenate([jnp.cos(ar), jnp.cos(ar), jnp.cos(ac), jnp.cos(ac)], axis=1)
    s1b = jnp.concatenate([-jnp.sin(ar), zq, -jnp.sin(ac), zq], axis=1)
    s2b = jnp.concatenate([zq, jnp.sin(ar), zq, jnp.sin(ac)], axis=1)
    return ca, s1a, s2a, cb, s1b, s2b


IN_TM = 256
IN_TN = 256
GAIN_QA, GAIN_KA, GAIN_QB, GAIN_KB = range(4)
HEAD_KINDS = ([(GAIN_QA, "a")] * 12 + [(GAIN_KA, "a")] * 12 + [None] * 12
              + [(GAIN_QB, "b")] * 8 + [(GAIN_KB, "b")] * 2 + [None] * 2)


def _in_proj_kernel(x_ref, gmix_ref, w_ref, gain_ref, ca_ref, s1a_ref, s2a_ref,
                    cb_ref, s1b_ref, s2b_ref, o_ref):
    x = x_ref[...]
    ms = jnp.mean(x * x, axis=-1, keepdims=True)
    h = (x * lax.rsqrt(ms + EPS) * gmix_ref[...]).astype(BF16)
    rot = {"a": (ca_ref, s1a_ref, s2a_ref, ROT_A // 2), "b": (cb_ref, s1b_ref, s2b_ref, AXIAL_HALF // 2)}
    heads_per_dot = IN_TN // HEAD_DIM
    for j in range(QKV_WIDTH // IN_TN):
        t = jnp.dot(h, w_ref[:, j * IN_TN:(j + 1) * IN_TN], preferred_element_type=F32)
        for hh in range(heads_per_dot):
            head = j * heads_per_dot + hh
            y = t[:, hh * HEAD_DIM:(hh + 1) * HEAD_DIM]
            if HEAD_KINDS[head] is not None:
                gain, kind = HEAD_KINDS[head]
                c_ref, s1_ref, s2_ref, sh = rot[kind]
                y = y * lax.rsqrt(jnp.mean(y * y, axis=-1, keepdims=True) + EPS) * gain_ref[gain:gain + 1, :]
                y = (y * c_ref[...]
                     + pltpu.roll(y, HEAD_DIM - sh, 1) * s1_ref[...]
                     + pltpu.roll(y, sh, 1) * s2_ref[...])
            o_ref[head] = y.astype(BF16)


def _in_proj(x2, gmix, w_qkv, gains, tables):
    t_tok = x2.shape[0]
    nseq = SEQ // IN_TM
    tab_spec = pl.BlockSpec((IN_TM, HEAD_DIM), lambda i: (i % nseq, 0))
    return pl.pallas_call(
        _in_proj_kernel,
        grid=(t_tok // IN_TM,),
        in_specs=[pl.BlockSpec((IN_TM, D_MODEL), lambda i: (i, 0)),
                  _resident(gmix.shape), _resident(w_qkv.shape), _resident(gains.shape),
                  tab_spec, tab_spec, tab_spec, tab_spec, tab_spec, tab_spec],
        out_specs=pl.BlockSpec((N_QKV_HEADS, IN_TM, HEAD_DIM), lambda i: (0, i, 0)),
        out_shape=jax.ShapeDtypeStruct((N_QKV_HEADS, t_tok, HEAD_DIM), BF16),
        compiler_params=pltpu.CompilerParams(dimension_semantics=("arbitrary",),
                                             vmem_limit_bytes=VMEM_LIMIT),
        name="in_proj",
    )(x2, gmix, w_qkv, gains, *tables)


ATT_TQ = 256
NT = (((1,), (1,)), ((), ()))


def _attn_b_kernel(q_ref, k_ref, v_ref, o_ref):
    k = k_ref[0]
    v = v_ref[0]
    for hh in range(4):
        s = lax.dot_general(q_ref[hh], k, NT, preferred_element_type=F32) * SCALE
        m = jnp.max(s, axis=-1, keepdims=True)
        p = jnp.exp(s - m)
        l = jnp.sum(p, axis=-1, keepdims=True)
        o = jnp.dot(p.astype(BF16), v, preferred_element_type=F32) / l
        o_ref[:, hh * HEAD_DIM:(hh + 1) * HEAD_DIM] = o.astype(BF16)


def _attn_b(qkv, n_batch):
    t_tok = qkv.shape[1]
    nq = SEQ // ATT_TQ
    return pl.pallas_call(
        _attn_b_kernel,
        grid=(n_batch, 2, nq),
        in_specs=[pl.BlockSpec((4, ATT_TQ, HEAD_DIM), lambda b, g, qi: (QB0 // 4 + g, b * nq + qi, 0)),
                  pl.BlockSpec((1, SEQ, HEAD_DIM), lambda b, g, qi: (KB0 + g, b, 0)),
                  pl.BlockSpec((1, SEQ, HEAD_DIM), lambda b, g, qi: (VB0 + g, b, 0))],
        out_specs=pl.BlockSpec((ATT_TQ, 4 * HEAD_DIM), lambda b, g, qi: (b * nq + qi, g)),
        out_shape=jax.ShapeDtypeStruct((t_tok, 8 * HEAD_DIM), BF16),
        compiler_params=pltpu.CompilerParams(
            dimension_semantics=("arbitrary", "arbitrary", "arbitrary"), vmem_limit_bytes=VMEM_LIMIT),
        name="attn_b",
    )(qkv, qkv, qkv)


def _key_window(dil):
    span = A_HALF * dil
    span = -(-span // LANES) * LANES
    return min(SEQ, ATT_TQ + 2 * span), span


def _attn_a_kernel(q0_ref, q1_ref, q2_ref, k0_ref, k1_ref, k2_ref, v0_ref, v1_ref, v2_ref, o_ref):
    q0pos = pl.program_id(2) * ATT_TQ
    scores, values = [], []
    for q_ref, k_ref, v_ref, dil in ((q0_ref, k0_ref, v0_ref, A_DILATIONS[0]),
                                     (q1_ref, k1_ref, v1_ref, A_DILATIONS[1]),
                                     (q2_ref, k2_ref, v2_ref, A_DILATIONS[2])):
        width, span = _key_window(dil)
        if width < SEQ:
            start = pl.multiple_of(jnp.clip(q0pos - span, 0, SEQ - width), LANES)
            kk = k_ref[0, pl.ds(start, width), :]
            vv = v_ref[0, pl.ds(start, width), :]
        else:
            start = 0
            kk = k_ref[0]
            vv = v_ref[0]
        s = lax.dot_general(q_ref[0], kk, NT, preferred_element_type=F32) * SCALE
        rel = (lax.broadcasted_iota(jnp.int32, (ATT_TQ, width), 1)
               - lax.broadcasted_iota(jnp.int32, (ATT_TQ, width), 0)) + (start - q0pos)
        valid = jnp.abs(rel) <= A_HALF * dil
        if dil > 1:
            valid = jnp.logical_and(valid, (rel & (dil - 1)) == 0)
        scores.append(jnp.where(valid, s, NEG))
        values.append(vv)
    m = functools.reduce(jnp.maximum, [jnp.max(s, axis=-1, keepdims=True) for s in scores])
    l = jnp.zeros((ATT_TQ, 1), F32)
    acc = jnp.zeros((ATT_TQ, HEAD_DIM), F32)
    for s, vv in zip(scores, values):
        p = jnp.exp(s - m)
        l = l + jnp.sum(p, axis=-1, keepdims=True)
        acc = acc + jnp.dot(p.astype(BF16), vv, preferred_element_type=F32)
    o_ref[...] = (acc / l).astype(BF16)


def _attn_a(qkv, n_batch):
    t_tok = qkv.shape[1]
    nq = SEQ // ATT_TQ
    g4 = A_HEADS_PER_GROUP

    def q_spec(g):
        return pl.BlockSpec((1, ATT_TQ, HEAD_DIM), lambda b, hh, qi: (QA0 + g4 * g + hh, b * nq + qi, 0))

    def kv_spec(base, g):
        return pl.BlockSpec((1, SEQ, HEAD_DIM), lambda b, hh, qi: (base + g4 * g + hh, b, 0))

    return pl.pallas_call(
        _attn_a_kernel,
        grid=(n_batch, g4, nq),
        in_specs=[q_spec(0), q_spec(1), q_spec(2),
                  kv_spec(KA0, 0), kv_spec(KA0, 1), kv_spec(KA0, 2),
                  kv_spec(VA0, 0), kv_spec(VA0, 1), kv_spec(VA0, 2)],
        out_specs=pl.BlockSpec((ATT_TQ, HEAD_DIM), lambda b, hh, qi: (b * nq + qi, hh)),
        out_shape=jax.ShapeDtypeStruct((t_tok, g4 * HEAD_DIM), BF16),
        compiler_params=pltpu.CompilerParams(
            dimension_semantics=("arbitrary", "arbitrary", "arbitrary"), vmem_limit_bytes=VMEM_LIMIT),
        name="attn_a",
    )(*([qkv] * 9))


MG_TM = 256
MG_CH = 512


def _sigmoid(z):
    return 1.0 / (1.0 + jnp.exp(-z))


def _merge_kernel(x_ref, oa_ref, ob_ref, gmix_ref, wg_ref, wa_ref, wb_ref, wo_ref, o_ref, m_scr):
    x = x_ref[...]
    ms = jnp.mean(x * x, axis=-1, keepdims=True)
    h = (x * lax.rsqrt(ms + EPS) * gmix_ref[...]).astype(BF16)
    oa = oa_ref[...]
    ob = ob_ref[...]
    for c in range(D_MODEL // MG_CH):
        lo, hi = c * MG_CH, (c + 1) * MG_CH
        ga = jnp.dot(h, wg_ref[:, lo:hi], preferred_element_type=F32)
        gb = jnp.dot(h, wg_ref[:, D_MODEL + lo:D_MODEL + hi], preferred_element_type=F32)
        pa = jnp.dot(oa, wa_ref[:, lo:hi], preferred_element_type=F32)
        pb = jnp.dot(ob, wb_ref[:, lo:hi], preferred_element_type=F32)
        m_scr[:, lo:hi] = (_sigmoid(ga) * pa + _sigmoid(gb) * pb).astype(BF16)
    o_ref[...] = x + jnp.dot(m_scr[...], wo_ref[...], preferred_element_type=F32)


def _resident(shape):
    nd = len(shape)
    return pl.BlockSpec(shape, lambda i: (0,) * nd, pipeline_mode=pl.Buffered(1))


def _merge(x2, oa, ob, gmix, w_g, w_a, w_b, w_o):
    t_tok = x2.shape[0]
    return pl.pallas_call(
        _merge_kernel,
        grid=(t_tok // MG_TM,),
        in_specs=[pl.BlockSpec((MG_TM, D_MODEL), lambda i: (i, 0)),
                  pl.BlockSpec((MG_TM, oa.shape[1]), lambda i: (i, 0)),
                  pl.BlockSpec((MG_TM, ob.shape[1]), lambda i: (i, 0)),
                  _resident(gmix.shape), _resident(w_g.shape), _resident(w_a.shape),
                  _resident(w_b.shape), _resident(w_o.shape)],
        out_specs=pl.BlockSpec((MG_TM, D_MODEL), lambda i: (i, 0)),
        out_shape=jax.ShapeDtypeStruct((t_tok, D_MODEL), F32),
        scratch_shapes=[pltpu.VMEM((MG_TM, D_MODEL), BF16)],
        compiler_params=pltpu.CompilerParams(dimension_semantics=("arbitrary",),
                                             vmem_limit_bytes=VMEM_LIMIT),
        name="merge",
    )(x2, oa, ob, gmix, w_g, w_a, w_b, w_o)


RT_TM = 256
CAND_ROWS = PEER_TOPK + (PEER_TOPK - 1) * 8


def _top16_rows(s, key_f, val_ref, idx_ref):
    for r in range(PEER_TOPK):
        m = jnp.max(s, axis=0, keepdims=True)
        idx = jnp.min(jnp.where(s == m, key_f, float(PEER_NKEYS)), axis=0, keepdims=True)
        val_ref[r:r + 1, :] = m
        idx_ref[r:r + 1, :] = idx
        s = jnp.where(key_f == idx, -jnp.inf, s)


PERM8 = (0, 4, 2, 6, 1, 5, 3, 7)


def _coef_row(q):
    return (q // 8) * 8 + PERM8[q % 8]


def _peer_route_kernel(x_ref, gffn_ref, wpq_ref, sk_ref, e_ref, g_ref,
                       q_scr, v1_scr, i1_scr, v2_scr, i2_scr, cv_scr, ce_scr, et_scr):
    x = x_ref[...]
    ms = jnp.mean(x * x, axis=-1, keepdims=True)
    h = (x * lax.rsqrt(ms + EPS) * gffn_ref[...]).astype(BF16)
    q = jnp.dot(h, wpq_ref[...], preferred_element_type=F32)
    for hc in range(2 * PEER_HEADS):
        q_scr[hc] = q[:, hc * LANES:(hc + 1) * LANES].astype(BF16)

    key_f = lax.broadcasted_iota(jnp.int32, (PEER_NKEYS, LANES), 0).astype(F32)
    row = lax.broadcasted_iota(jnp.int32, (CAND_ROWS, LANES), 0)
    ca = jnp.where(row < PEER_TOPK, 0, 1 + ((row - PEER_TOPK) >> 3))
    cb = jnp.where(row < PEER_TOPK, row, (row - PEER_TOPK) & 7)
    flat_f = (ca * PEER_TOPK + cb).astype(F32)
    cand_ok = (ca + 1) * (cb + 1) <= PEER_TOPK

    def head_body(hd, carry):
        for lh in range(RT_TM // LANES):
            for c, (val_ref, idx_ref) in enumerate(((v1_scr, i1_scr), (v2_scr, i2_scr))):
                s = lax.dot_general(sk_ref[hd * 2 + c], q_scr[hd * 2 + c, pl.ds(lh * LANES, LANES), :],
                                    NT, preferred_element_type=F32)
                _top16_rows(s, key_f, val_ref, idx_ref)
            v1, i1 = v1_scr[...], i1_scr[...]
            v2, i2 = v2_scr[...], i2_scr[...]
            cand = [v1[0:1] + v2]
            cexp = [i1[0:1] * float(PEER_NKEYS) + i2]
            for a in range(1, PEER_TOPK):
                cand.append(v1[a:a + 1] + v2[0:8])
                cexp.append(i1[a:a + 1] * float(PEER_NKEYS) + i2[0:8])
            cand = jnp.where(cand_ok, jnp.concatenate(cand, axis=0), -jnp.inf)
            cexp = jnp.concatenate(cexp, axis=0)
            for r in range(PEER_TOPK):
                m = jnp.max(cand, axis=0, keepdims=True)
                sel = jnp.min(jnp.where(cand == m, flat_f, 1e9), axis=0, keepdims=True)
                hit = flat_f == sel
                cv_scr[_coef_row(r):_coef_row(r) + 1, :] = m
                ce_scr[r:r + 1, :] = jnp.max(jnp.where(hit, cexp, -1.0), axis=0, keepdims=True)
                cand = jnp.where(hit, -jnp.inf, cand)
            cv = cv_scr[...]
            w = jnp.exp(cv - cv[0:1])
            w = w / jnp.sum(w, axis=0, keepdims=True)
            rows = pl.ds(pl.multiple_of(hd * PEER_TOPK, PEER_TOPK), PEER_TOPK)
            et_scr[rows, lh * LANES:(lh + 1) * LANES] = ce_scr[...]
            g_ref[rows, lh * LANES:(lh + 1) * LANES] = w
        return carry

    lax.fori_loop(0, PEER_HEADS, head_body, 0)
    e_ref[...] = et_scr[...].T.astype(jnp.int32)


def _peer_route(x1, gffn, w_pq, sk):
    t_tok = x1.shape[0]
    small = [pltpu.VMEM((PEER_TOPK, LANES), F32) for _ in range(6)]
    return pl.pallas_call(
        _peer_route_kernel,
        grid=(t_tok // RT_TM,),
        in_specs=[pl.BlockSpec((RT_TM, D_MODEL), lambda i: (i, 0)),
                  _resident(gffn.shape), _resident(w_pq.shape), _resident(sk.shape)],
        out_specs=[pl.BlockSpec((RT_TM, PEER_SEL), lambda i: (i, 0)),
                   pl.BlockSpec((PEER_SEL, RT_TM), lambda i: (0, i))],
        out_shape=[jax.ShapeDtypeStruct((t_tok, PEER_SEL), jnp.int32),
                   jax.ShapeDtypeStruct((PEER_SEL, t_tok), F32)],
        scratch_shapes=[pltpu.VMEM((2 * PEER_HEADS, RT_TM, LANES), BF16)] + small
                       + [pltpu.VMEM((PEER_SEL, RT_TM), F32)],
        compiler_params=pltpu.CompilerParams(dimension_semantics=("arbitrary",),
                                             vmem_limit_bytes=VMEM_LIMIT),
        name="peer_route",
    )(x1, gffn, w_pq, sk)


MX_TB = 128
MX_SLOTS = 8
MX_AHEAD = 7


def _merge8(ps, sub):
    lo4, lo2, lo1 = (sub & 4) == 0, (sub & 2) == 0, (sub & 1) == 0
    q = [jnp.where(lo4, ps[2 * i], ps[2 * i + 1])
         + pltpu.roll(jnp.where(lo4, ps[2 * i + 1], ps[2 * i]), 4, 0) for i in range(4)]
    r = [jnp.where(lo2, q[2 * i] + pltpu.roll(q[2 * i], 6, 0), q[2 * i + 1] + pltpu.roll(q[2 * i + 1], 2, 0))
         for i in range(2)]
    return jnp.where(lo1, r[0] + pltpu.roll(r[0], 7, 0), r[1] + pltpu.roll(r[1], 1, 0))


def _peer_mix_kernel(e_ref, en_ref, x_ref, g_ref, gffn_ref, tab_ref, y_ref, buf, sem, h_scr, m_scr,
                     c_scr):
    step_i = pl.program_id(0)
    x = x_ref[...].reshape(MX_TB // 8, ROW_CHUNKS, 8, LANES)
    ss = jnp.sum(jnp.sum(x * x, axis=3, keepdims=True), axis=1, keepdims=True)
    h = x * lax.rsqrt(ss * (1.0 / D_MODEL) + EPS) * gffn_ref[...]
    h_scr[...] = h.reshape(MX_TB * ROW_CHUNKS, LANES)

    sub = lax.broadcasted_iota(jnp.int32, (8, LANES), 0)
    lane = lax.broadcasted_iota(jnp.int32, (PEER_SEL, MX_TB), 1)

    def row_copy(idx_ref, tok, k, slot):
        return pltpu.make_async_copy(tab_ref.at[idx_ref[tok, k]], buf.at[slot, k], sem.at[slot])

    def wait_rows(slot):
        pltpu.make_async_copy(tab_ref.at[pl.ds(0, PEER_SEL)], buf.at[slot], sem.at[slot]).wait()

    def step(s, r, idx_ref, itok, do_u, do_c, do_v):
        slot_v, slot_u, slot_i = r % MX_SLOTS, (r + 2) % MX_SLOTS, (r + MX_AHEAD) % MX_SLOTS
        par = r % 2
        if do_u:
            wait_rows(slot_u)
            hrow = h_scr[_row_slice(s + 2), :]
            hlo, hhi = hrow[0:8], hrow[8:16]
        if do_c:
            a = jnp.sum(m_scr[1 - par], axis=1, keepdims=True)
            gate = jnp.sum(jnp.where(lane == s + 1, g_ref[...], 0.0), axis=1, keepdims=True)
            coef = 0.5 * a * (1.0 + lax.erf(a * INV_SQRT2)) * gate
        zero = jnp.zeros((ROW_CHUNKS, LANES), F32)
        accs = [zero, zero, zero, zero]
        merged = []
        for j in range(PEER_SEL // 8):
            parts = []
            for kk in range(8):
                q = 8 * j + kk
                if idx_ref is not None:
                    row_copy(idx_ref, itok, q, slot_i).start(priority=q % 2)
                if do_u:
                    u = buf[slot_u, q, 0:ROW_CHUNKS, :].astype(F32)
                    parts.append(u[0:8] * hlo + u[8:16] * hhi)
                if do_v:
                    v = buf[slot_v, q, ROW_CHUNKS:2 * ROW_CHUNKS, :].astype(F32)
                    accs[kk % 4] = accs[kk % 4] + c_scr[par, pl.ds(_coef_row(q), 1), :] * v
            if do_u:
                merged.append(_merge8(parts, sub))
        if do_v:
            rows = _row_slice(s)
            y_ref[rows, :] = x_ref[rows, :] + ((accs[0] + accs[1]) + (accs[2] + accs[3]))
        if do_u:
            for j in range(PEER_SEL // 8):
                m_scr[par, 8 * j:8 * j + 8, :] = merged[j]
        if do_c:
            c_scr[1 - par] = jnp.broadcast_to(coef, (PEER_SEL, LANES))

    @pl.when(step_i == 0)
    def _():
        for tok in range(MX_AHEAD):
            for q in range(PEER_SEL):
                row_copy(e_ref, tok, q, tok).start(priority=q % 2)

    step(-2, MX_SLOTS - 2, None, None, True, False, False)
    step(-1, MX_SLOTS - 1, None, None, True, True, False)

    def body(it, carry):
        for r in range(MX_SLOTS):
            s = it * MX_SLOTS + r
            step(s, r, e_ref, s + MX_AHEAD, True, True, True)
        return carry

    lax.fori_loop(0, MX_TB // MX_SLOTS - 1, body, 0)
    for s in range(MX_TB - MX_SLOTS, MX_TB):
        nxt = s + MX_AHEAD - MX_TB
        src, tok = (e_ref, s + MX_AHEAD) if nxt < 0 else (en_ref, nxt)
        step(s, s % MX_SLOTS, src, tok, s + 2 < MX_TB, s + 1 < MX_TB, True)

    @pl.when(step_i == pl.num_programs(0) - 1)
    def _():
        for tok in range(MX_AHEAD):
            wait_rows(tok)


def _peer_mix(x1, e_t, g_t, gffn, table):
    t_tok = x1.shape[0]
    assert MX_SLOTS & (MX_SLOTS - 1) == 0 and MX_TB % MX_SLOTS == 0 and MX_AHEAD <= 8
    rows = MX_TB * ROW_CHUNKS
    last8 = t_tok // 8 - 1
    y_rows = pl.pallas_call(
        _peer_mix_kernel,
        grid=(t_tok // MX_TB,),
        in_specs=[pl.BlockSpec((MX_TB, PEER_SEL), lambda i: (i, 0), memory_space=pltpu.SMEM),
                  pl.BlockSpec((8, PEER_SEL), lambda i: (jnp.minimum((i + 1) * (MX_TB // 8), last8), 0),
                               memory_space=pltpu.SMEM),
                  pl.BlockSpec((rows, LANES), lambda i: (i, 0)),
                  pl.BlockSpec((PEER_SEL, MX_TB), lambda i: (0, i)),
                  pl.BlockSpec((ROW_CHUNKS, 1, LANES), lambda i: (0, 0, 0)),
                  pl.BlockSpec(memory_space=pl.ANY)],
        out_specs=pl.BlockSpec((rows, LANES), lambda i: (i, 0)),
        out_shape=jax.ShapeDtypeStruct((t_tok * ROW_CHUNKS, LANES), F32),
        scratch_shapes=[pltpu.VMEM((MX_SLOTS, PEER_SEL, 2 * ROW_CHUNKS, LANES), BF16),
                        pltpu.SemaphoreType.DMA((MX_SLOTS,)),
                        pltpu.VMEM((rows, LANES), F32),
                        pltpu.VMEM((2, PEER_SEL, LANES), F32),
                        pltpu.VMEM((2, PEER_SEL, LANES), F32)],
        compiler_params=pltpu.CompilerParams(dimension_semantics=("arbitrary",),
                                             vmem_limit_bytes=VMEM_LIMIT),
        name="peer_mix",
    )(e_t, e_t, _to_rows(x1), g_t, gffn.reshape(ROW_CHUNKS, 1, LANES), table)
    return _from_rows(y_rows, t_tok)


def _prepare(g_mix, w_in, qn_a, kn_a, qn_b, kn_b, w_br_a, w_br_b, w_out, g_ffn, w_pq, sub_keys,
             u_emb, v_emb):
    return dict(
        gmix=g_mix.reshape(1, D_MODEL),
        w_qkv=w_in[:, :QKV_WIDTH].astype(BF16),
        w_g=w_in[:, QKV_WIDTH:].astype(BF16),
        gains=jnp.stack([qn_a, kn_a, qn_b, kn_b]),
        tables=_rope_tables(),
        w_a=w_br_a.astype(BF16), w_b=w_br_b.astype(BF16), w_o=w_out.astype(BF16),
        gffn=g_ffn.reshape(1, D_MODEL),
        w_pq=w_pq.astype(BF16),
        sk=sub_keys.reshape(2 * PEER_HEADS, PEER_NKEYS, LANES).astype(BF16),
        table=_pack_table(u_emb, v_emb),
    )


def _layer(x, p):
    n_batch, length, d = x.shape
    assert length == SEQ and d == D_MODEL
    x2 = x.reshape(n_batch * length, d)
    qkv = _in_proj(x2, p["gmix"], p["w_qkv"], p["gains"], p["tables"])
    ob = _attn_b(qkv, n_batch)
    oa = _attn_a(qkv, n_batch)
    x1 = _merge(x2, oa, ob, p["gmix"], p["w_g"], p["w_a"], p["w_b"], p["w_o"])
    e_t, g_t = _peer_route(x1, p["gffn"], p["w_pq"], p["sk"])
    y = _peer_mix(x1, e_t, g_t, p["gffn"], p["table"])
    return y.reshape(n_batch, length, d)


def kernel(x_prompt, x_sample, g_mix, w_in, qn_a, kn_a, qn_b, kn_b, w_br_a, w_br_b, w_out, g_ffn,
           w_pq, sub_keys, u_emb, v_emb):
    y_prompt, y_sample = x_prompt, x_sample
    for l in range(g_mix.shape[0]):
        p = _prepare(g_mix[l], w_in[l], qn_a[l], kn_a[l], qn_b[l], kn_b[l], w_br_a[l], w_br_b[l],
                     w_out[l], g_ffn[l], w_pq[l], sub_keys[l], u_emb[l], v_emb[l])
        y_prompt = _layer(y_prompt, p)
        y_sample = _layer(y_sample, p)
    return (y_prompt, y_sample)
```

```python
import functools
import math

import jax
import jax.numpy as jnp
from jax import lax
from jax.experimental import pallas as pl
from jax.experimental.pallas import tpu as pltpu

F32 = jnp.float32
BF16 = jnp.bfloat16

D_MODEL = 2048
SEQ = 2048
HEAD_DIM = 128
EPS = 1e-6
GRID_W = 64
SCALE = HEAD_DIM ** -0.5
A_HEADS_PER_GROUP = 4
A_DILATIONS = (1, 4, 16)
A_HALF = 64
ROPE_THETA_A = 500000.0
ROT_A = HEAD_DIM // 4
ROPE_THETA_B = 10000.0
AXIAL_HALF = HEAD_DIM // 2
QKV_WIDTH = 6144
QA0, KA0, VA0, QB0, KB0, VB0 = 0, 12, 24, 36, 44, 46
N_QKV_HEADS = 48
PEER_HEADS = 8
PEER_NKEYS = 128
PEER_TOPK = 16
PEER_SEL = PEER_HEADS * PEER_TOPK
ROW_CHUNKS = D_MODEL // 128
NEG = -1e30
INV_SQRT2 = 0.7071067811865476

LANES = 128
VMEM_LIMIT = 56 * 1024 * 1024


def _to_rows(x2):
    n = x2.shape[0]
    return x2.reshape(n // 8, 8, ROW_CHUNKS, LANES).transpose(0, 2, 1, 3).reshape(n * ROW_CHUNKS, LANES)


def _from_rows(r2, n):
    return r2.reshape(n // 8, ROW_CHUNKS, 8, LANES).transpose(0, 2, 1, 3).reshape(n, ROW_CHUNKS * LANES)


def _row_slice(i):
    return pl.ds((i >> 3) * (8 * ROW_CHUNKS) + (i & 7), ROW_CHUNKS, stride=8)


PACK_BLOCKS = 8


def _pack_table_kernel(u_ref, v_ref, o_ref):
    for e in range(8 * PACK_BLOCKS):
        o_ref[e, 0:ROW_CHUNKS, :] = u_ref[_row_slice(e), :].astype(BF16)
        o_ref[e, ROW_CHUNKS:2 * ROW_CHUNKS, :] = v_ref[_row_slice(e), :].astype(BF16)


def _pack_table(u_emb, v_emb):
    n = u_emb.shape[0]
    experts = 8 * PACK_BLOCKS
    rows = experts * ROW_CHUNKS
    return pl.pallas_call(
        _pack_table_kernel,
        grid=(n // experts,),
        in_specs=[pl.BlockSpec((rows, LANES), lambda i: (i, 0)),
                  pl.BlockSpec((rows, LANES), lambda i: (i, 0))],
        out_specs=pl.BlockSpec((experts, 2 * ROW_CHUNKS, LANES), lambda i: (i, 0, 0)),
        out_shape=jax.ShapeDtypeStruct((n, 2 * ROW_CHUNKS, LANES), BF16),
        name="pack_table",
    )(_to_rows(u_emb), _to_rows(v_emb))


def _rope_tables():
    pos = jnp.arange(SEQ, dtype=F32)
    ha = ROT_A // 2
    inv_a = ROPE_THETA_A ** (-(jnp.arange(0, ROT_A, 2, dtype=F32) / ROT_A))
    ang = pos[:, None] * inv_a[None, :]
    cos, sin = jnp.cos(ang), jnp.sin(ang)
    pad = jnp.zeros((SEQ, HEAD_DIM - ROT_A), F32)
    zh = jnp.zeros((SEQ, ha), F32)
    ca = jnp.concatenate([cos, cos, pad + 1.0], axis=1)
    s1a = jnp.concatenate([-sin, zh, pad], axis=1)
    s2a = jnp.concatenate([zh, sin, pad], axis=1)
    rows = SEQ // GRID_W
    row_ids = jnp.repeat(jnp.arange(rows), GRID_W).astype(F32)
    col_ids = jnp.tile(jnp.arange(GRID_W), rows).astype(F32)
    inv_b = ROPE_THETA_B ** (-(jnp.arange(0, AXIAL_HALF, 2, dtype=F32) / AXIAL_HALF))
    ar = row_ids[:, None] * inv_b[None, :]
    ac = col_ids[:, None] * inv_b[None, :]
    zq = jnp.zeros_like(ar)
    cb = jnp.concatenate([jnp.cos(ar), jnp.cos(ar), jnp.cos(ac), jnp.cos(ac)], axis=1)
    s1b = jnp.concatenate([-jnp.sin(ar), zq, -jnp.sin(ac), zq], axis=1)
    s2b = jnp.concatenate([zq, jnp.sin(ar), zq, jnp.sin(ac)], axis=1)
    return ca, s1a, s2a, cb, s1b, s2b


IN_TM = 256
IN_TN = 256
GAIN_QA, GAIN_KA, GAIN_QB, GAIN_KB = range(4)
HEAD_KINDS = ([(GAIN_QA, "a")] * 12 + [(GAIN_KA, "a")] * 12 + [None] * 12
              + [(GAIN_QB, "b")] * 8 + [(GAIN_KB, "b")] * 2 + [None] * 2)


def _in_proj_kernel(x_ref, gmix_ref, w_ref, gain_ref, ca_ref, s1a_ref, s2a_ref,
                    cb_ref, s1b_ref, s2b_ref, o_ref):
    x = x_ref[...]
    ms = jnp.mean(x * x, axis=-1, keepdims=True)
    h = (x * lax.rsqrt(ms + EPS) * gmix_ref[...]).astype(BF16)
    rot = {"a": (ca_ref, s1a_ref, s2a_ref, ROT_A // 2), "b": (cb_ref, s1b_ref, s2b_ref, AXIAL_HALF // 2)}
    heads_per_dot = IN_TN // HEAD_DIM
    for j in range(QKV_WIDTH // IN_TN):
        t = jnp.dot(h, w_ref[:, j * IN_TN:(j + 1) * IN_TN], preferred_element_type=F32)
        for hh in range(heads_per_dot):
            head = j * heads_per_dot + hh
            y = t[:, hh * HEAD_DIM:(hh + 1) * HEAD_DIM]
            if HEAD_KINDS[head] is not None:
                gain, kind = HEAD_KINDS[head]
                c_ref, s1_ref, s2_ref, sh = rot[kind]
                y = y * lax.rsqrt(jnp.mean(y * y, axis=-1, keepdims=True) + EPS) * gain_ref[gain:gain + 1, :]
                y = (y * c_ref[...]
                     + pltpu.roll(y, HEAD_DIM - sh, 1) * s1_ref[...]
                     + pltpu.roll(y, sh, 1) * s2_ref[...])
            o_ref[head] = y.astype(BF16)


def _in_proj(x2, gmix, w_qkv, gains, tables):
    t_tok = x2.shape[0]
    nseq = SEQ // IN_TM
    tab_spec = pl.BlockSpec((IN_TM, HEAD_DIM), lambda i: (i % nseq, 0))
    return pl.pallas_call(
        _in_proj_kernel,
        grid=(t_tok // IN_TM,),
        in_specs=[pl.BlockSpec((IN_TM, D_MODEL), lambda i: (i, 0)),
                  _resident(gmix.shape), _resident(w_qkv.shape), _resident(gains.shape),
                  tab_spec, tab_spec, tab_spec, tab_spec, tab_spec, tab_spec],
        out_specs=pl.BlockSpec((N_QKV_HEADS, IN_TM, HEAD_DIM), lambda i: (0, i, 0)),
        out_shape=jax.ShapeDtypeStruct((N_QKV_HEADS, t_tok, HEAD_DIM), BF16),
        compiler_params=pltpu.CompilerParams(dimension_semantics=("arbitrary",),
                                             vmem_limit_bytes=VMEM_LIMIT),
        name="in_proj",
    )(x2, gmix, w_qkv, gains, *tables)


ATT_TQ = 256
NT = (((1,), (1,)), ((), ()))


def _attn_b_kernel(q_ref, k_ref, v_ref, o_ref):
    k = k_ref[0]
    v = v_ref[0]
    for hh in range(4):
        s = lax.dot_general(q_ref[hh], k, NT, preferred_element_type=F32) * SCALE
        m = jnp.max(s, axis=-1, keepdims=True)
        p = jnp.exp(s - m)
        l = jnp.sum(p, axis=-1, keepdims=True)
        o = jnp.dot(p.astype(BF16), v, preferred_element_type=F32) / l
        o_ref[:, hh * HEAD_DIM:(hh + 1) * HEAD_DIM] = o.astype(BF16)


def _attn_b(qkv, n_batch):
    t_tok = qkv.shape[1]
    nq = SEQ // ATT_TQ
    return pl.pallas_call(
        _attn_b_kernel,
        grid=(n_batch, 2, nq),
        in_specs=[pl.BlockSpec((4, ATT_TQ, HEAD_DIM), lambda b, g, qi: (QB0 // 4 + g, b * nq + qi, 0)),
                  pl.BlockSpec((1, SEQ, HEAD_DIM), lambda b, g, qi: (KB0 + g, b, 0)),
                  pl.BlockSpec((1, SEQ, HEAD_DIM), lambda b, g, qi: (VB0 + g, b, 0))],
        out_specs=pl.BlockSpec((ATT_TQ, 4 * HEAD_DIM), lambda b, g, qi: (b * nq + qi, g)),
        out_shape=jax.ShapeDtypeStruct((t_tok, 8 * HEAD_DIM), BF16),
        compiler_params=pltpu.CompilerParams(
            dimension_semantics=("arbitrary", "arbitrary", "arbitrary"), vmem_limit_bytes=VMEM_LIMIT),
        name="attn_b",
    )(qkv, qkv, qkv)


def _key_window(dil):
    span = A_HALF * dil
    span = -(-span // LANES) * LANES
    return min(SEQ, ATT_TQ + 2 * span), span


def _attn_a_kernel(q0_ref, q1_ref, q2_ref, k0_ref, k1_ref, k2_ref, v0_ref, v1_ref, v2_ref, o_ref):
    q0pos = pl.program_id(2) * ATT_TQ
    scores, values = [], []
    for q_ref, k_ref, v_ref, dil in ((q0_ref, k0_ref, v0_ref, A_DILATIONS[0]),
                                     (q1_ref, k1_ref, v1_ref, A_DILATIONS[1]),
                                     (q2_ref, k2_ref, v2_ref, A_DILATIONS[2])):
        width, span = _key_window(dil)
        if width < SEQ:
            start = pl.multiple_of(jnp.clip(q0pos - span, 0, SEQ - width), LANES)
            kk = k_ref[0, pl.ds(start, width), :]
            vv = v_ref[0, pl.ds(start, width), :]
        else:
            start = 0
            kk = k_ref[0]
            vv = v_ref[0]
        s = lax.dot_general(q_ref[0], kk, NT, preferred_element_type=F32) * SCALE
        rel = (lax.broadcasted_iota(jnp.int32, (ATT_TQ, width), 1)
               - lax.broadcasted_iota(jnp.int32, (ATT_TQ, width), 0)) + (start - q0pos)
        valid = jnp.abs(rel) <= A_HALF * dil
        if dil > 1:
            valid = jnp.logical_and(valid, (rel & (dil - 1)) == 0)
        scores.append(jnp.where(valid, s, NEG))
        values.append(vv)
    m = functools.reduce(jnp.maximum, [jnp.max(s, axis=-1, keepdims=True) for s in scores])
    l = jnp.zeros((ATT_TQ, 1), F32)
    acc = jnp.zeros((ATT_TQ, HEAD_DIM), F32)
    for s, vv in zip(scores, values):
        p = jnp.exp(s - m)
        l = l + jnp.sum(p, axis=-1, keepdims=True)
        acc = acc + jnp.dot(p.astype(BF16), vv, preferred_element_type=F32)
    o_ref[...] = (acc / l).astype(BF16)


def _attn_a(qkv, n_batch):
    t_tok = qkv.shape[1]
    nq = SEQ // ATT_TQ
    g4 = A_HEADS_PER_GROUP

    def q_spec(g):
        return pl.BlockSpec((1, ATT_TQ, HEAD_DIM), lambda b, hh, qi: (QA0 + g4 * g + hh, b * nq + qi, 0))

    def kv_spec(base, g):
        return pl.BlockSpec((1, SEQ, HEAD_DIM), lambda b, hh, qi: (base + g4 * g + hh, b, 0))

    return pl.pallas_call(
        _attn_a_kernel,
        grid=(n_batch, g4, nq),
        in_specs=[q_spec(0), q_spec(1), q_spec(2),
                  kv_spec(KA0, 0), kv_spec(KA0, 1), kv_spec(KA0, 2),
                  kv_spec(VA0, 0), kv_spec(VA0, 1), kv_spec(VA0, 2)],
        out_specs=pl.BlockSpec((ATT_TQ, HEAD_DIM), lambda b, hh, qi: (b * nq + qi, hh)),
        out_shape=jax.ShapeDtypeStruct((t_tok, g4 * HEAD_DIM), BF16),
        compiler_params=pltpu.CompilerParams(
            dimension_semantics=("arbitrary", "arbitrary", "arbitrary"), vmem_limit_bytes=VMEM_LIMIT),
        name="attn_a",
    )(*([qkv] * 9))


MG_TM = 256
MG_CH = 512


def _sigmoid(z):
    return 1.0 / (1.0 + jnp.exp(-z))


def _merge_kernel(x_ref, oa_ref, ob_ref, gmix_ref, wg_ref, wa_ref, wb_ref, wo_ref, o_ref, m_scr):
    x = x_ref[...]
    ms = jnp.mean(x * x, axis=-1, keepdims=True)
    h = (x * lax.rsqrt(ms + EPS) * gmix_ref[...]).astype(BF16)
    oa = oa_ref[...]
    ob = ob_ref[...]
    for c in range(D_MODEL // MG_CH):
        lo, hi = c * MG_CH, (c + 1) * MG_CH
        ga = jnp.dot(h, wg_ref[:, lo:hi], preferred_element_type=F32)
        gb = jnp.dot(h, wg_ref[:, D_MODEL + lo:D_MODEL + hi], preferred_element_type=F32)
        pa = jnp.dot(oa, wa_ref[:, lo:hi], preferred_element_type=F32)
        pb = jnp.dot(ob, wb_ref[:, lo:hi], preferred_element_type=F32)
        m_scr[:, lo:hi] = (_sigmoid(ga) * pa + _sigmoid(gb) * pb).astype(BF16)
    o_ref[...] = x + jnp.dot(m_scr[...], wo_ref[...], preferred_element_type=F32)


def _resident(shape):
    nd = len(shape)
    return pl.BlockSpec(shape, lambda i: (0,) * nd, pipeline_mode=pl.Buffered(1))


def _merge(x2, oa, ob, gmix, w_g, w_a, w_b, w_o):
    t_tok = x2.shape[0]
    return pl.pallas_call(
        _merge_kernel,
        grid=(t_tok // MG_TM,),
        in_specs=[pl.BlockSpec((MG_TM, D_MODEL), lambda i: (i, 0)),
                  pl.BlockSpec((MG_TM, oa.shape[1]), lambda i: (i, 0)),
                  pl.BlockSpec((MG_TM, ob.shape[1]), lambda i: (i, 0)),
                  _resident(gmix.shape), _resident(w_g.shape), _resident(w_a.shape),
                  _resident(w_b.shape), _resident(w_o.shape)],
        out_specs=pl.BlockSpec((MG_TM, D_MODEL), lambda i: (i, 0)),
        out_shape=jax.ShapeDtypeStruct((t_tok, D_MODEL), F32),
        scratch_shapes=[pltpu.VMEM((MG_TM, D_MODEL), BF16)],
        compiler_params=pltpu.CompilerParams(dimension_semantics=("arbitrary",),
                                             vmem_limit_bytes=VMEM_LIMIT),
        name="merge",
    )(x2, oa, ob, gmix, w_g, w_a, w_b, w_o)


RT_TM = 256


def _top16_rows(s, key_f, val_ref, idx_ref):
    for r in range(PEER_TOPK):
        m = jnp.max(s, axis=0, keepdims=True)
        idx = jnp.min(jnp.where(s == m, key_f, float(PEER_NKEYS)), axis=0, keepdims=True)
        val_ref[r:r + 1, :] = m
        idx_ref[r:r + 1, :] = idx
        s = jnp.where(key_f == idx, -jnp.inf, s)


PERM8 = (0, 4, 2, 6, 1, 5, 3, 7)


def _coef_row(q):
    return (q // 8) * 8 + PERM8[q % 8]


def _peer_route_kernel(x_ref, gffn_ref, wpq_ref, sk_ref, e_ref, g_ref,
                       q_scr, v1_scr, i1_scr, v2_scr, i2_scr, cv_scr, ce_scr, et_scr):
    x = x_ref[...]
    ms = jnp.mean(x * x, axis=-1, keepdims=True)
    h = (x * lax.rsqrt(ms + EPS) * gffn_ref[...]).astype(BF16)
    q = jnp.dot(h, wpq_ref[...], preferred_element_type=F32)
    for hc in range(2 * PEER_HEADS):
        q_scr[hc] = q[:, hc * LANES:(hc + 1) * LANES].astype(BF16)

    key_f = lax.broadcasted_iota(jnp.int32, (PEER_NKEYS, LANES), 0).astype(F32)
    sub = lax.broadcasted_iota(jnp.int32, (8, LANES), 0)
    zero8 = jnp.zeros_like(sub)
    ca = jnp.concatenate([zero8, zero8, zero8 + 1, zero8 + 2, 3 + (sub >> 2), 5 + (sub >> 1), 8 + sub], axis=0)
    cb = jnp.concatenate([sub, sub + 8, sub, sub, sub & 3, sub & 1, zero8], axis=0)
    pairs_ok = jnp.concatenate([zero8 == 0] * 5 + [sub < 6, zero8 == 0], axis=0)
    cand_ok = jnp.logical_and((ca + 1) * (cb + 1) <= PEER_TOPK, pairs_ok)
    flat_f = jnp.where(cand_ok, ca * PEER_TOPK + cb, -1).astype(F32)

    def cand_tiles(t1, t2, combine):
        lo2, hi2 = t2[0:8], t2[8:16]
        return jnp.concatenate([
            combine(t1[0:1], lo2), combine(t1[0:1], hi2), combine(t1[1:2], lo2), combine(t1[2:3], lo2),
            combine(jnp.where(sub < 4, t1[3:4], t1[4:5]), jnp.where(sub < 4, lo2, pltpu.roll(lo2, 4, 0))),
            combine(jnp.where(sub < 2, t1[5:6], jnp.where(sub < 4, t1[6:7], t1[7:8])),
                    jnp.where((sub & 1) == 0, t2[0:1], t2[1:2])),
            combine(t1[8:16], t2[0:1])], axis=0)

    def head_body(hd, carry):
        for lh in range(RT_TM // LANES):
            for c, (val_ref, idx_ref) in enumerate(((v1_scr, i1_scr), (v2_scr, i2_scr))):
                s = lax.dot_general(sk_ref[hd * 2 + c], q_scr[hd * 2 + c, pl.ds(lh * LANES, LANES), :],
                                    NT, preferred_element_type=F32)
                _top16_rows(s, key_f, val_ref, idx_ref)
            v1, i1 = v1_scr[...], i1_scr[...]
            v2, i2 = v2_scr[...], i2_scr[...]
            cand = jnp.where(cand_ok, cand_tiles(v1, v2, lambda x, y: x + y), -jnp.inf)
            cexp = cand_tiles(i1, i2, lambda x, y: x * float(PEER_NKEYS) + y)
            for r in range(PEER_TOPK):
                m = jnp.max(cand, axis=0, keepdims=True)
                sel = jnp.min(jnp.where(cand == m, flat_f, 1e9), axis=0, keepdims=True)
                hit = flat_f == sel
                cv_scr[_coef_row(r):_coef_row(r) + 1, :] = m
                ce_scr[r:r + 1, :] = jnp.max(jnp.where(hit, cexp, -1.0), axis=0, keepdims=True)
                cand = jnp.where(hit, -jnp.inf, cand)
            cv = cv_scr[...]
            w = jnp.exp(cv - cv[0:1])
            w = w / jnp.sum(w, axis=0, keepdims=True)
            rows = pl.ds(pl.multiple_of(hd * PEER_TOPK, PEER_TOPK), PEER_TOPK)
            et_scr[rows, lh * LANES:(lh + 1) * LANES] = ce_scr[...]
            g_ref[rows, lh * LANES:(lh + 1) * LANES] = w
        return carry

    lax.fori_loop(0, PEER_HEADS, head_body, 0)
    e_ref[...] = et_scr[...].T.astype(jnp.int32)


def _peer_route(x1, gffn, w_pq, sk):
    t_tok = x1.shape[0]
    small = [pltpu.VMEM((PEER_TOPK, LANES), F32) for _ in range(6)]
    return pl.pallas_call(
        _peer_route_kernel,
        grid=(t_tok // RT_TM,),
        in_specs=[pl.BlockSpec((RT_TM, D_MODEL), lambda i: (i, 0)),
                  _resident(gffn.shape), _resident(w_pq.shape), _resident(sk.shape)],
        out_specs=[pl.BlockSpec((RT_TM, PEER_SEL), lambda i: (i, 0)),
                   pl.BlockSpec((PEER_SEL, RT_TM), lambda i: (0, i))],
        out_shape=[jax.ShapeDtypeStruct((t_tok, PEER_SEL), jnp.int32),
                   jax.ShapeDtypeStruct((PEER_SEL, t_tok), F32)],
        scratch_shapes=[pltpu.VMEM((2 * PEER_HEADS, RT_TM, LANES), BF16)] + small
                       + [pltpu.VMEM((PEER_SEL, RT_TM), F32)],
        compiler_params=pltpu.CompilerParams(dimension_semantics=("arbitrary",),
                                             vmem_limit_bytes=VMEM_LIMIT),
        name="peer_route",
    )(x1, gffn, w_pq, sk)


MX_TB = 128
MX_SLOTS = 8
MX_AHEAD = 7


def _merge8(ps, sub):
    lo4, lo2, lo1 = (sub & 4) == 0, (sub & 2) == 0, (sub & 1) == 0
    q = [jnp.where(lo4, ps[2 * i], ps[2 * i + 1])
         + pltpu.roll(jnp.where(lo4, ps[2 * i + 1], ps[2 * i]), 4, 0) for i in range(4)]
    r = [jnp.where(lo2, q[2 * i] + pltpu.roll(q[2 * i], 6, 0), q[2 * i + 1] + pltpu.roll(q[2 * i + 1], 2, 0))
         for i in range(2)]
    return jnp.where(lo1, r[0] + pltpu.roll(r[0], 7, 0), r[1] + pltpu.roll(r[1], 1, 0))


def _peer_mix_kernel(e_ref, en_ref, x_ref, g_ref, gffn_ref, tab_ref, y_ref, buf, sem, h_scr, m_scr,
                     c_scr):
    step_i = pl.program_id(0)
    x = x_ref[...].reshape(MX_TB // 8, ROW_CHUNKS, 8, LANES)
    ss = jnp.sum(jnp.sum(x * x, axis=3, keepdims=True), axis=1, keepdims=True)
    h = x * lax.rsqrt(ss * (1.0 / D_MODEL) + EPS) * gffn_ref[...]
    h_scr[...] = h.reshape(MX_TB * ROW_CHUNKS, LANES)

    sub = lax.broadcasted_iota(jnp.int32, (8, LANES), 0)
    lane = lax.broadcasted_iota(jnp.int32, (PEER_SEL, MX_TB), 1)

    def row_copy(idx_ref, tok, k, slot):
        return pltpu.make_async_copy(tab_ref.at[idx_ref[tok, k]], buf.at[slot, k], sem.at[slot])

    def wait_rows(slot):
        pltpu.make_async_copy(tab_ref.at[pl.ds(0, PEER_SEL)], buf.at[slot], sem.at[slot]).wait()

    def step(s, r, idx_ref, itok, do_u, do_c, do_v):
        slot_v, slot_u, slot_i = r % MX_SLOTS, (r + 2) % MX_SLOTS, (r + MX_AHEAD) % MX_SLOTS
        par = r % 2
        if do_u:
            wait_rows(slot_u)
            hrow = h_scr[_row_slice(s + 2), :]
            hlo, hhi = hrow[0:8], hrow[8:16]
        if do_c:
            a = jnp.sum(m_scr[1 - par], axis=1, keepdims=True)
            gate = jnp.sum(jnp.where(lane == s + 1, g_ref[...], 0.0), axis=1, keepdims=True)
            coef = 0.5 * a * (1.0 + lax.erf(a * INV_SQRT2)) * gate
        zero = jnp.zeros((ROW_CHUNKS, LANES), F32)
        accs = [zero, zero, zero, zero]
        merged = []
        for j in range(PEER_SEL // 8):
            parts = []
            for kk in range(8):
                q = 8 * j + kk
                if idx_ref is not None:
                    row_copy(idx_ref, itok, q, slot_i).start(priority=q % 2)
                if do_u:
                    u = buf[slot_u, q, 0:ROW_CHUNKS, :].astype(F32)
                    parts.append(u[0:8] * hlo + u[8:16] * hhi)
                if do_v:
                    v = buf[slot_v, q, ROW_CHUNKS:2 * ROW_CHUNKS, :].astype(F32)
                    accs[kk % 4] = accs[kk % 4] + c_scr[par, pl.ds(_coef_row(q), 1), :] * v
            if do_u:
                merged.append(_merge8(parts, sub))
        if do_v:
            rows = _row_slice(s)
            y_ref[rows, :] = x_ref[rows, :] + ((accs[0] + accs[1]) + (accs[2] + accs[3]))
        if do_u:
            for j in range(PEER_SEL // 8):
                m_scr[par, 8 * j:8 * j + 8, :] = merged[j]
        if do_c:
            c_scr[1 - par] = jnp.broadcast_to(coef, (PEER_SEL, LANES))

    @pl.when(step_i == 0)
    def _():
        for tok in range(MX_AHEAD):
            for q in range(PEER_SEL):
                row_copy(e_ref, tok, q, tok).start(priority=q % 2)

    step(-2, MX_SLOTS - 2, None, None, True, False, False)
    step(-1, MX_SLOTS - 1, None, None, True, True, False)

    def body(it, carry):
        for r in range(MX_SLOTS):
            s = it * MX_SLOTS + r
            step(s, r, e_ref, s + MX_AHEAD, True, True, True)
        return carry

    lax.fori_loop(0, MX_TB // MX_SLOTS - 1, body, 0)
    for s in range(MX_TB - MX_SLOTS, MX_TB):
        nxt = s + MX_AHEAD - MX_TB
        src, tok = (e_ref, s + MX_AHEAD) if nxt < 0 else (en_ref, nxt)
        step(s, s % MX_SLOTS, src, tok, s + 2 < MX_TB, s + 1 < MX_TB, True)

    @pl.when(step_i == pl.num_programs(0) - 1)
    def _():
        for tok in range(MX_AHEAD):
            wait_rows(tok)


def _peer_mix(x1, e_t, g_t, gffn, table):
    t_tok = x1.shape[0]
    assert MX_SLOTS & (MX_SLOTS - 1) == 0 and MX_TB % MX_SLOTS == 0 and MX_AHEAD <= 8
    rows = MX_TB * ROW_CHUNKS
    last8 = t_tok // 8 - 1
    y_rows = pl.pallas_call(
        _peer_mix_kernel,
        grid=(t_tok // MX_TB,),
        in_specs=[pl.BlockSpec((MX_TB, PEER_SEL), lambda i: (i, 0), memory_space=pltpu.SMEM),
                  pl.BlockSpec((8, PEER_SEL), lambda i: (jnp.minimum((i + 1) * (MX_TB // 8), last8), 0),
                               memory_space=pltpu.SMEM),
                  pl.BlockSpec((rows, LANES), lambda i: (i, 0)),
                  pl.BlockSpec((PEER_SEL, MX_TB), lambda i: (0, i)),
                  pl.BlockSpec((ROW_CHUNKS, 1, LANES), lambda i: (0, 0, 0)),
                  pl.BlockSpec(memory_space=pl.ANY)],
        out_specs=pl.BlockSpec((rows, LANES), lambda i: (i, 0)),
        out_shape=jax.ShapeDtypeStruct((t_tok * ROW_CHUNKS, LANES), F32),
        scratch_shapes=[pltpu.VMEM((MX_SLOTS, PEER_SEL, 2 * ROW_CHUNKS, LANES), BF16),
                        pltpu.SemaphoreType.DMA((MX_SLOTS,)),
                        pltpu.VMEM((rows, LANES), F32),
                        pltpu.VMEM((2, PEER_SEL, LANES), F32),
                        pltpu.VMEM((2, PEER_SEL, LANES), F32)],
        compiler_params=pltpu.CompilerParams(dimension_semantics=("arbitrary",),
                                             vmem_limit_bytes=VMEM_LIMIT),
        name="peer_mix",
    )(e_t, e_t, _to_rows(x1), g_t, gffn.reshape(ROW_CHUNKS, 1, LANES), table)
    return _from_rows(y_rows, t_tok)


def _prepare(g_mix, w_in, qn_a, kn_a, qn_b, kn_b, w_br_a, w_br_b, w_out, g_ffn, w_pq, sub_keys,
             u_emb, v_emb):
    return dict(
        gmix=g_mix.reshape(1, D_MODEL),
        w_qkv=w_in[:, :QKV_WIDTH].astype(BF16),
        w_g=w_in[:, QKV_WIDTH:].astype(BF16),
        gains=jnp.stack([qn_a, kn_a, qn_b, kn_b]),
        tables=_rope_tables(),
        w_a=w_br_a.astype(BF16), w_b=w_br_b.astype(BF16), w_o=w_out.astype(BF16),
        gffn=g_ffn.reshape(1, D_MODEL),
        w_pq=w_pq.astype(BF16),
        sk=sub_keys.reshape(2 * PEER_HEADS, PEER_NKEYS, LANES).astype(BF16),
        table=_pack_table(u_emb, v_emb),
    )


def _layer(x, p):
    n_batch, length, d = x.shape
    assert length == SEQ and d == D_MODEL
    x2 = x.reshape(n_batch * length, d)
    qkv = _in_proj(x2, p["gmix"], p["w_qkv"], p["gains"], p["tables"])
    ob = _attn_b(qkv, n_batch)
    oa = _attn_a(qkv, n_batch)
    x1 = _merge(x2, oa, ob, p["gmix"], p["w_g"], p["w_a"], p["w_b"], p["w_o"])
    e_t, g_t = _peer_route(x1, p["gffn"], p["w_pq"], p["sk"])
    y = _peer_mix(x1, e_t, g_t, p["gffn"], p["table"])
    return y.reshape(n_batch, length, d)


def kernel(x_prompt, x_sample, g_mix, w_in, qn_a, kn_a, qn_b, kn_b, w_br_a, w_br_b, w_out, g_ffn,
           w_pq, sub_keys, u_emb, v_emb):
    y_prompt, y_sample = x_prompt, x_sample
    for l in range(g_mix.shape[0]):
        p = _prepare(g_mix[l], w_in[l], qn_a[l], kn_a[l], qn_b[l], kn_b[l], w_br_a[l], w_br_b[l],
                     w_out[l], g_ffn[l], w_pq[l], sub_keys[l], u_emb[l], v_emb[l])
        y_prompt = _layer(y_prompt, p)
        y_sample = _layer(y_sample, p)
    return (y_prompt, y_sample)
```

```python
import functools
import math

import jax
import jax.numpy as jnp
from jax import lax
from jax.experimental import pallas as pl
from jax.experimental.pallas import tpu as pltpu

F32 = jnp.float32
BF16 = jnp.bfloat16

D_MODEL = 2048
SEQ = 2048
HEAD_DIM = 128
EPS = 1e-6
GRID_W = 64
SCALE = HEAD_DIM ** -0.5
A_HEADS_PER_GROUP = 4
A_DILATIONS = (1, 4, 16)
A_HALF = 64
ROPE_THETA_A = 500000.0
ROT_A = HEAD_DIM // 4
ROPE_THETA_B = 10000.0
AXIAL_HALF = HEAD_DIM // 2
QKV_WIDTH = 6144
QA0, KA0, VA0, QB0, KB0, VB0 = 0, 12, 24, 36, 44, 46
N_QKV_HEADS = 48
PEER_HEADS = 8
PEER_NKEYS = 128
PEER_TOPK = 16
PEER_SEL = PEER_HEADS * PEER_TOPK
ROW_CHUNKS = D_MODEL // 128
NEG = -1e30
INV_SQRT2 = 0.7071067811865476

LANES = 128
VMEM_LIMIT = 56 * 1024 * 1024


def _to_rows(x2):
    n = x2.shape[0]
    return x2.reshape(n // 8, 8, ROW_CHUNKS, LANES).transpose(0, 2, 1, 3).reshape(n * ROW_CHUNKS, LANES)


def _from_rows(r2, n):
    return r2.reshape(n // 8, ROW_CHUNKS, 8, LANES).transpose(0, 2, 1, 3).reshape(n, ROW_CHUNKS * LANES)


def _row_slice(i):
    return pl.ds((i >> 3) * (8 * ROW_CHUNKS) + (i & 7), ROW_CHUNKS, stride=8)


PACK_BLOCKS = 8


def _pack_table_kernel(u_ref, v_ref, o_ref):
    for e in range(8 * PACK_BLOCKS):
        o_ref[e, 0:ROW_CHUNKS, :] = u_ref[_row_slice(e), :].astype(BF16)
        o_ref[e, ROW_CHUNKS:2 * ROW_CHUNKS, :] = v_ref[_row_slice(e), :].astype(BF16)


def _pack_table(u_emb, v_emb):
    n = u_emb.shape[0]
    experts = 8 * PACK_BLOCKS
    rows = experts * ROW_CHUNKS
    return pl.pallas_call(
        _pack_table_kernel,
        grid=(n // experts,),
        in_specs=[pl.BlockSpec((rows, LANES), lambda i: (i, 0)),
                  pl.BlockSpec((rows, LANES), lambda i: (i, 0))],
        out_specs=pl.BlockSpec((experts, 2 * ROW_CHUNKS, LANES), lambda i: (i, 0, 0)),
        out_shape=jax.ShapeDtypeStruct((n, 2 * ROW_CHUNKS, LANES), BF16),
        name="pack_table",
    )(_to_rows(u_emb), _to_rows(v_emb))


def _rope_tables():
    pos = jnp.arange(SEQ, dtype=F32)
    ha = ROT_A // 2
    inv_a = ROPE_THETA_A ** (-(jnp.arange(0, ROT_A, 2, dtype=F32) / ROT_A))
    ang = pos[:, None] * inv_a[None, :]
    cos, sin = jnp.cos(ang), jnp.sin(ang)
    pad = jnp.zeros((SEQ, HEAD_DIM - ROT_A), F32)
    zh = jnp.zeros((SEQ, ha), F32)
    ca = jnp.concatenate([cos, cos, pad + 1.0], axis=1)
    s1a = jnp.concatenate([-sin, zh, pad], axis=1)
    s2a = jnp.concatenate([zh, sin, pad], axis=1)
    rows = SEQ // GRID_W
    row_ids = jnp.repeat(jnp.arange(rows), GRID_W).astype(F32)
    col_ids = jnp.tile(jnp.arange(GRID_W), rows).astype(F32)
    inv_b = ROPE_THETA_B ** (-(jnp.arange(0, AXIAL_HALF, 2, dtype=F32) / AXIAL_HALF))
    ar = row_ids[:, None] * inv_b[None, :]
    ac = col_ids[:, None] * inv_b[None, :]
    zq = jnp.zeros_like(ar)
    cb = jnp.concatenate([jnp.cos(ar), jnp.cos(ar), jnp.cos(ac), jnp.cos(ac)], axis=1)
    s1b = jnp.concatenate([-jnp.sin(ar), zq, -jnp.sin(ac), zq], axis=1)
    s2b = jnp.concatenate([zq, jnp.sin(ar), zq, jnp.sin(ac)], axis=1)
    return ca, s1a, s2a, cb, s1b, s2b


IN_TM = 256
IN_TN = 256
GAIN_QA, GAIN_KA, GAIN_QB, GAIN_KB = range(4)
HEAD_KINDS = ([(GAIN_QA, "a")] * 12 + [(GAIN_KA, "a")] * 12 + [None] * 12
              + [(GAIN_QB, "b")] * 8 + [(GAIN_KB, "b")] * 2 + [None] * 2)


DIL_G2 = A_DILATIONS[2]
G2_HEADS = tuple(base + 2 * A_HEADS_PER_GROUP + i for base in (QA0, KA0, VA0) for i in range(A_HEADS_PER_GROUP))
G2_STEPS = SEQ // DIL_G2


def _in_proj_kernel(x_ref, gmix_ref, w_ref, gain_ref, ca_ref, s1a_ref, s2a_ref,
                    cb_ref, s1b_ref, s2b_ref, o_ref, d_ref, y_scr):
    x = x_ref[...]
    ms = jnp.mean(x * x, axis=-1, keepdims=True)
    h = (x * lax.rsqrt(ms + EPS) * gmix_ref[...]).astype(BF16)
    rot = {"a": (ca_ref, s1a_ref, s2a_ref, ROT_A // 2), "b": (cb_ref, s1b_ref, s2b_ref, AXIAL_HALF // 2)}
    heads_per_dot = IN_TN // HEAD_DIM
    for j in range(QKV_WIDTH // IN_TN):
        t = jnp.dot(h, w_ref[:, j * IN_TN:(j + 1) * IN_TN], preferred_element_type=F32)
        for hh in range(heads_per_dot):
            head = j * heads_per_dot + hh
            y = t[:, hh * HEAD_DIM:(hh + 1) * HEAD_DIM]
            if HEAD_KINDS[head] is not None:
                gain, kind = HEAD_KINDS[head]
                c_ref, s1_ref, s2_ref, sh = rot[kind]
                y = y * lax.rsqrt(jnp.mean(y * y, axis=-1, keepdims=True) + EPS) * gain_ref[gain:gain + 1, :]
                y = (y * c_ref[...]
                     + pltpu.roll(y, HEAD_DIM - sh, 1) * s1_ref[...]
                     + pltpu.roll(y, sh, 1) * s2_ref[...])
            o_ref[head] = y.astype(BF16)
            if head in G2_HEADS:
                y_scr[...] = y
                for res in range(DIL_G2):
                    d_ref[G2_HEADS.index(head), 0, :, res * HEAD_DIM:(res + 1) * HEAD_DIM] = (
                        y_scr[pl.ds(res, IN_TM // DIL_G2, stride=DIL_G2), :].astype(BF16))


def _in_proj(x2, gmix, w_qkv, gains, tables):
    t_tok = x2.shape[0]
    nseq = SEQ // IN_TM
    tab_spec = pl.BlockSpec((IN_TM, HEAD_DIM), lambda i: (i % nseq, 0))
    return pl.pallas_call(
        _in_proj_kernel,
        grid=(t_tok // IN_TM,),
        in_specs=[pl.BlockSpec((IN_TM, D_MODEL), lambda i: (i, 0)),
                  _resident(gmix.shape), _resident(w_qkv.shape), _resident(gains.shape),
                  tab_spec, tab_spec, tab_spec, tab_spec, tab_spec, tab_spec],
        out_specs=[pl.BlockSpec((N_QKV_HEADS, IN_TM, HEAD_DIM), lambda i: (0, i, 0)),
                   pl.BlockSpec((len(G2_HEADS), 1, IN_TM // DIL_G2, DIL_G2 * HEAD_DIM),
                                lambda i: (0, i // nseq, i % nseq, 0))],
        out_shape=[jax.ShapeDtypeStruct((N_QKV_HEADS, t_tok, HEAD_DIM), BF16),
                   jax.ShapeDtypeStruct((len(G2_HEADS), t_tok // SEQ, G2_STEPS, DIL_G2 * HEAD_DIM), BF16)],
        scratch_shapes=[pltpu.VMEM((IN_TM, HEAD_DIM), F32)],
        compiler_params=pltpu.CompilerParams(dimension_semantics=("arbitrary",),
                                             vmem_limit_bytes=VMEM_LIMIT),
        name="in_proj",
    )(x2, gmix, w_qkv, gains, *tables)


ATT_TQ = 256
NT = (((1,), (1,)), ((), ()))


def _attn_b_kernel(q_ref, k_ref, v_ref, o_ref):
    k = k_ref[0]
    v = v_ref[0]
    for hh in range(4):
        s = lax.dot_general(q_ref[hh], k, NT, preferred_element_type=F32) * SCALE
        m = jnp.max(s, axis=-1, keepdims=True)
        p = jnp.exp(s - m)
        l = jnp.sum(p, axis=-1, keepdims=True)
        o = jnp.dot(p.astype(BF16), v, preferred_element_type=F32) / l
        o_ref[:, hh * HEAD_DIM:(hh + 1) * HEAD_DIM] = o.astype(BF16)


def _attn_b(qkv, n_batch):
    t_tok = qkv.shape[1]
    nq = SEQ // ATT_TQ
    return pl.pallas_call(
        _attn_b_kernel,
        grid=(n_batch, 2, nq),
        in_specs=[pl.BlockSpec((4, ATT_TQ, HEAD_DIM), lambda b, g, qi: (QB0 // 4 + g, b * nq + qi, 0)),
                  pl.BlockSpec((1, SEQ, HEAD_DIM), lambda b, g, qi: (KB0 + g, b, 0)),
                  pl.BlockSpec((1, SEQ, HEAD_DIM), lambda b, g, qi: (VB0 + g, b, 0))],
        out_specs=pl.BlockSpec((ATT_TQ, 4 * HEAD_DIM), lambda b, g, qi: (b * nq + qi, g)),
        out_shape=jax.ShapeDtypeStruct((t_tok, 8 * HEAD_DIM), BF16),
        compiler_params=pltpu.CompilerParams(
            dimension_semantics=("arbitrary", "arbitrary", "arbitrary"), vmem_limit_bytes=VMEM_LIMIT),
        name="attn_b",
    )(qkv, qkv, qkv)


def _key_window(dil):
    span = A_HALF * dil
    span = -(-span // LANES) * LANES
    return min(SEQ, ATT_TQ + 2 * span), span


def _attn_g2_kernel(q_ref, k_ref, v_ref, o_ref, lse_ref):
    i = lax.broadcasted_iota(jnp.int32, (G2_STEPS, G2_STEPS), 0)
    j = lax.broadcasted_iota(jnp.int32, (G2_STEPS, G2_STEPS), 1)
    band = jnp.abs(i - j) <= A_HALF
    for res in range(DIL_G2):
        cols = slice(res * HEAD_DIM, (res + 1) * HEAD_DIM)
        s = lax.dot_general(q_ref[0, 0, :, cols], k_ref[0, 0, :, cols], NT, preferred_element_type=F32) * SCALE
        s = jnp.where(band, s, NEG)
        m = jnp.max(s, axis=-1, keepdims=True)
        p = jnp.exp(s - m)
        l = jnp.sum(p, axis=-1, keepdims=True)
        o = jnp.dot(p.astype(BF16), v_ref[0, 0, :, cols], preferred_element_type=F32) / l
        rows = pl.ds(res, G2_STEPS, stride=DIL_G2)
        o_ref[rows, :] = o
        lse_ref[rows, :] = jnp.broadcast_to(m + jnp.log(l), (G2_STEPS, HEAD_DIM))


def _attn_g2(d16, n_batch):
    g4 = A_HEADS_PER_GROUP
    width = DIL_G2 * HEAD_DIM

    def spec(base):
        return pl.BlockSpec((1, 1, G2_STEPS, width), lambda b, hh: (base + hh, b, 0, 0))

    out_spec = pl.BlockSpec((SEQ, HEAD_DIM), lambda b, hh: (b, hh))
    out_shape = jax.ShapeDtypeStruct((n_batch * SEQ, g4 * HEAD_DIM), F32)
    return pl.pallas_call(
        _attn_g2_kernel,
        grid=(n_batch, g4),
        in_specs=[spec(0), spec(g4), spec(2 * g4)],
        out_specs=[out_spec, out_spec],
        out_shape=[out_shape, out_shape],
        compiler_params=pltpu.CompilerParams(dimension_semantics=("arbitrary", "arbitrary"),
                                             vmem_limit_bytes=VMEM_LIMIT),
        name="attn_g2",
    )(d16, d16, d16)


def _attn_a_kernel(q0_ref, q1_ref, k0_ref, k1_ref, v0_ref, v1_ref, o2_ref, lse2_ref, o_ref):
    q0pos = pl.program_id(2) * ATT_TQ
    scores, values = [], []
    for q_ref, k_ref, v_ref, dil in ((q0_ref, k0_ref, v0_ref, A_DILATIONS[0]),
                                     (q1_ref, k1_ref, v1_ref, A_DILATIONS[1])):
        width, span = _key_window(dil)
        if width < SEQ:
            start = pl.multiple_of(jnp.clip(q0pos - span, 0, SEQ - width), LANES)
            kk = k_ref[0, pl.ds(start, width), :]
            vv = v_ref[0, pl.ds(start, width), :]
        else:
            start = 0
            kk = k_ref[0]
            vv = v_ref[0]
        s = lax.dot_general(q_ref[0], kk, NT, preferred_element_type=F32) * SCALE
        rel = (lax.broadcasted_iota(jnp.int32, (ATT_TQ, width), 1)
               - lax.broadcasted_iota(jnp.int32, (ATT_TQ, width), 0)) + (start - q0pos)
        valid = jnp.abs(rel) <= A_HALF * dil
        if dil > 1:
            valid = jnp.logical_and(valid, (rel & (dil - 1)) == 0)
        scores.append(jnp.where(valid, s, NEG))
        values.append(vv)
    lse2 = lse2_ref[:, 0:1]
    m = functools.reduce(jnp.maximum, [jnp.max(s, axis=-1, keepdims=True) for s in scores] + [lse2])
    w2 = jnp.exp(lse2 - m)
    l = w2
    acc = o2_ref[...] * w2
    for s, vv in zip(scores, values):
        p = jnp.exp(s - m)
        l = l + jnp.sum(p, axis=-1, keepdims=True)
        acc = acc + jnp.dot(p.astype(BF16), vv, preferred_element_type=F32)
    o_ref[...] = (acc / l).astype(BF16)


def _attn_a(qkv, o2, lse2, n_batch):
    t_tok = qkv.shape[1]
    nq = SEQ // ATT_TQ
    g4 = A_HEADS_PER_GROUP

    def q_spec(g):
        return pl.BlockSpec((1, ATT_TQ, HEAD_DIM), lambda b, hh, qi: (QA0 + g4 * g + hh, b * nq + qi, 0))

    def kv_spec(base, g):
        return pl.BlockSpec((1, SEQ, HEAD_DIM), lambda b, hh, qi: (base + g4 * g + hh, b, 0))

    tile_spec = pl.BlockSpec((ATT_TQ, HEAD_DIM), lambda b, hh, qi: (b * nq + qi, hh))
    return pl.pallas_call(
        _attn_a_kernel,
        grid=(n_batch, g4, nq),
        in_specs=[q_spec(0), q_spec(1), kv_spec(KA0, 0), kv_spec(KA0, 1), kv_spec(VA0, 0), kv_spec(VA0, 1),
                  tile_spec, tile_spec],
        out_specs=tile_spec,
        out_shape=jax.ShapeDtypeStruct((t_tok, g4 * HEAD_DIM), BF16),
        compiler_params=pltpu.CompilerParams(
            dimension_semantics=("arbitrary", "arbitrary", "arbitrary"), vmem_limit_bytes=VMEM_LIMIT),
        name="attn_a",
    )(*([qkv] * 6), o2, lse2)


MG_TM = 256
MG_CH = 512


def _sigmoid(z):
    return 1.0 / (1.0 + jnp.exp(-z))


def _merge_kernel(x_ref, oa_ref, ob_ref, gmix_ref, wg_ref, wa_ref, wb_ref, wo_ref, o_ref, m_scr):
    x = x_ref[...]
    ms = jnp.mean(x * x, axis=-1, keepdims=True)
    h = (x * lax.rsqrt(ms + EPS) * gmix_ref[...]).astype(BF16)
    oa = oa_ref[...]
    ob = ob_ref[...]
    for c in range(D_MODEL // MG_CH):
        lo, hi = c * MG_CH, (c + 1) * MG_CH
        ga = jnp.dot(h, wg_ref[:, lo:hi], preferred_element_type=F32)
        gb = jnp.dot(h, wg_ref[:, D_MODEL + lo:D_MODEL + hi], preferred_element_type=F32)
        pa = jnp.dot(oa, wa_ref[:, lo:hi], preferred_element_type=F32)
        pb = jnp.dot(ob, wb_ref[:, lo:hi], preferred_element_type=F32)
        m_scr[:, lo:hi] = (_sigmoid(ga) * pa + _sigmoid(gb) * pb).astype(BF16)
    o_ref[...] = x + jnp.dot(m_scr[...], wo_ref[...], preferred_element_type=F32)


def _resident(shape):
    nd = len(shape)
    return pl.BlockSpec(shape, lambda i: (0,) * nd, pipeline_mode=pl.Buffered(1))


def _merge(x2, oa, ob, gmix, w_g, w_a, w_b, w_o):
    t_tok = x2.shape[0]
    return pl.pallas_call(
        _merge_kernel,
        grid=(t_tok // MG_TM,),
        in_specs=[pl.BlockSpec((MG_TM, D_MODEL), lambda i: (i, 0)),
                  pl.BlockSpec((MG_TM, oa.shape[1]), lambda i: (i, 0)),
                  pl.BlockSpec((MG_TM, ob.shape[1]), lambda i: (i, 0)),
                  _resident(gmix.shape), _resident(w_g.shape), _resident(w_a.shape),
                  _resident(w_b.shape), _resident(w_o.shape)],
        out_specs=pl.BlockSpec((MG_TM, D_MODEL), lambda i: (i, 0)),
        out_shape=jax.ShapeDtypeStruct((t_tok, D_MODEL), F32),
        scratch_shapes=[pltpu.VMEM((MG_TM, D_MODEL), BF16)],
        compiler_params=pltpu.CompilerParams(dimension_semantics=("arbitrary",),
                                             vmem_limit_bytes=VMEM_LIMIT),
        name="merge",
    )(x2, oa, ob, gmix, w_g, w_a, w_b, w_o)


RT_TM = 256


def _top16_rows(s, key_f, val_ref, idx_ref):
    for r in range(PEER_TOPK):
        m = jnp.max(s, axis=0, keepdims=True)
        idx = jnp.min(jnp.where(s == m, key_f, float(PEER_NKEYS)), axis=0, keepdims=True)
        val_ref[r:r + 1, :] = m
        idx_ref[r:r + 1, :] = idx
        s = jnp.where(key_f == idx, -jnp.inf, s)


PERM8 = (0, 4, 2, 6, 1, 5, 3, 7)


def _coef_row(q):
    return (q // 8) * 8 + PERM8[q % 8]


def _peer_route_kernel(x_ref, gffn_ref, wpq_ref, sk_ref, e_ref, g_ref,
                       q_scr, v1_scr, i1_scr, v2_scr, i2_scr, cv_scr, ce_scr, et_scr):
    x = x_ref[...]
    ms = jnp.mean(x * x, axis=-1, keepdims=True)
    h = (x * lax.rsqrt(ms + EPS) * gffn_ref[...]).astype(BF16)
    q = jnp.dot(h, wpq_ref[...], preferred_element_type=F32)
    for hc in range(2 * PEER_HEADS):
        q_scr[hc] = q[:, hc * LANES:(hc + 1) * LANES].astype(BF16)

    key_f = lax.broadcasted_iota(jnp.int32, (PEER_NKEYS, LANES), 0).astype(F32)
    sub = lax.broadcasted_iota(jnp.int32, (8, LANES), 0)
    zero8 = jnp.zeros_like(sub)
    ca = jnp.concatenate([zero8, zero8, zero8 + 1, zero8 + 2, 3 + (sub >> 2), 5 + (sub >> 1), 8 + sub], axis=0)
    cb = jnp.concatenate([sub, sub + 8, sub, sub, sub & 3, sub & 1, zero8], axis=0)
    pairs_ok = jnp.concatenate([zero8 == 0] * 5 + [sub < 6, zero8 == 0], axis=0)
    cand_ok = jnp.logical_and((ca + 1) * (cb + 1) <= PEER_TOPK, pairs_ok)
    flat_f = jnp.where(cand_ok, ca * PEER_TOPK + cb, -1).astype(F32)

    def cand_tiles(t1, t2, combine):
        lo2, hi2 = t2[0:8], t2[8:16]
        return jnp.concatenate([
            combine(t1[0:1], lo2), combine(t1[0:1], hi2), combine(t1[1:2], lo2), combine(t1[2:3], lo2),
            combine(jnp.where(sub < 4, t1[3:4], t1[4:5]), jnp.where(sub < 4, lo2, pltpu.roll(lo2, 4, 0))),
            combine(jnp.where(sub < 2, t1[5:6], jnp.where(sub < 4, t1[6:7], t1[7:8])),
                    jnp.where((sub & 1) == 0, t2[0:1], t2[1:2])),
            combine(t1[8:16], t2[0:1])], axis=0)

    def head_body(hd, carry):
        for lh in range(RT_TM // LANES):
            for c, (val_ref, idx_ref) in enumerate(((v1_scr, i1_scr), (v2_scr, i2_scr))):
                s = lax.dot_general(sk_ref[hd * 2 + c], q_scr[hd * 2 + c, pl.ds(lh * LANES, LANES), :],
                                    NT, preferred_element_type=F32)
                _top16_rows(s, key_f, val_ref, idx_ref)
            v1, i1 = v1_scr[...], i1_scr[...]
            v2, i2 = v2_scr[...], i2_scr[...]
            cand = jnp.where(cand_ok, cand_tiles(v1, v2, lambda x, y: x + y), -jnp.inf)
            cexp = cand_tiles(i1, i2, lambda x, y: x * float(PEER_NKEYS) + y)
            for r in range(PEER_TOPK):
                m = jnp.max(cand, axis=0, keepdims=True)
                sel = jnp.min(jnp.where(cand == m, flat_f, 1e9), axis=0, keepdims=True)
                hit = flat_f == sel
                cv_scr[_coef_row(r):_coef_row(r) + 1, :] = m
                ce_scr[r:r + 1, :] = jnp.max(jnp.where(hit, cexp, -1.0), axis=0, keepdims=True)
                cand = jnp.where(hit, -jnp.inf, cand)
            cv = cv_scr[...]
            w = jnp.exp(cv - cv[0:1])
            w = w / jnp.sum(w, axis=0, keepdims=True)
            rows = pl.ds(pl.multiple_of(hd * PEER_TOPK, PEER_TOPK), PEER_TOPK)
            et_scr[rows, lh * LANES:(lh + 1) * LANES] = ce_scr[...]
            g_ref[rows, lh * LANES:(lh + 1) * LANES] = w
        return carry

    lax.fori_loop(0, PEER_HEADS, head_body, 0)
    e_ref[...] = et_scr[...].T.astype(jnp.int32)


def _peer_route(x1, gffn, w_pq, sk):
    t_tok = x1.shape[0]
    small = [pltpu.VMEM((PEER_TOPK, LANES), F32) for _ in range(6)]
    return pl.pallas_call(
        _peer_route_kernel,
        grid=(t_tok // RT_TM,),
        in_specs=[pl.BlockSpec((RT_TM, D_MODEL), lambda i: (i, 0)),
                  _resident(gffn.shape), _resident(w_pq.shape), _resident(sk.shape)],
        out_specs=[pl.BlockSpec((RT_TM, PEER_SEL), lambda i: (i, 0)),
                   pl.BlockSpec((PEER_SEL, RT_TM), lambda i: (0, i))],
        out_shape=[jax.ShapeDtypeStruct((t_tok, PEER_SEL), jnp.int32),
                   jax.ShapeDtypeStruct((PEER_SEL, t_tok), F32)],
        scratch_shapes=[pltpu.VMEM((2 * PEER_HEADS, RT_TM, LANES), BF16)] + small
                       + [pltpu.VMEM((PEER_SEL, RT_TM), F32)],
        compiler_params=pltpu.CompilerParams(dimension_semantics=("arbitrary",),
                                             vmem_limit_bytes=VMEM_LIMIT),
        name="peer_route",
    )(x1, gffn, w_pq, sk)


MX_TB = 128
MX_SLOTS = 8
MX_AHEAD = 7


def _merge8(ps, sub):
    lo4, lo2, lo1 = (sub & 4) == 0, (sub & 2) == 0, (sub & 1) == 0
    q = [jnp.where(lo4, ps[2 * i], ps[2 * i + 1])
         + pltpu.roll(jnp.where(lo4, ps[2 * i + 1], ps[2 * i]), 4, 0) for i in range(4)]
    r = [jnp.where(lo2, q[2 * i] + pltpu.roll(q[2 * i], 6, 0), q[2 * i + 1] + pltpu.roll(q[2 * i + 1], 2, 0))
         for i in range(2)]
    return jnp.where(lo1, r[0] + pltpu.roll(r[0], 7, 0), r[1] + pltpu.roll(r[1], 1, 0))


def _peer_mix_kernel(e_ref, en_ref, x_ref, g_ref, gffn_ref, tab_ref, y_ref, buf, sem, h_scr, m_scr,
                     c_scr):
    step_i = pl.program_id(0)
    x = x_ref[...].reshape(MX_TB // 8, ROW_CHUNKS, 8, LANES)
    ss = jnp.sum(jnp.sum(x * x, axis=3, keepdims=True), axis=1, keepdims=True)
    h = x * lax.rsqrt(ss * (1.0 / D_MODEL) + EPS) * gffn_ref[...]
    h_scr[...] = h.reshape(MX_TB * ROW_CHUNKS, LANES)

    sub = lax.broadcasted_iota(jnp.int32, (8, LANES), 0)
    lane = lax.broadcasted_iota(jnp.int32, (PEER_SEL, MX_TB), 1)

    def row_copy(idx_ref, tok, k, slot):
        return pltpu.make_async_copy(tab_ref.at[idx_ref[tok, k]], buf.at[slot, k], sem.at[slot])

    def wait_rows(slot):
        pltpu.make_async_copy(tab_ref.at[pl.ds(0, PEER_SEL)], buf.at[slot], sem.at[slot]).wait()

    def step(s, r, idx_ref, itok, do_u, do_c, do_v):
        slot_v, slot_u, slot_i = r % MX_SLOTS, (r + 2) % MX_SLOTS, (r + MX_AHEAD) % MX_SLOTS
        par = r % 2
        if do_u:
            wait_rows(slot_u)
            hrow = h_scr[_row_slice(s + 2), :]
            hlo, hhi = hrow[0:8], hrow[8:16]
        if do_c:
            a = jnp.sum(m_scr[1 - par], axis=1, keepdims=True)
            gate = jnp.sum(jnp.where(lane == s + 1, g_ref[...], 0.0), axis=1, keepdims=True)
            coef = 0.5 * a * (1.0 + lax.erf(a * INV_SQRT2)) * gate
        zero = jnp.zeros((ROW_CHUNKS, LANES), F32)
        accs = [zero, zero, zero, zero]
        merged = []
        for j in range(PEER_SEL // 8):
            parts = []
            for kk in range(8):
                q = 8 * j + kk
                if idx_ref is not None:
                    row_copy(idx_ref, itok, q, slot_i).start(priority=q % 2)
                if do_u:
                    u = buf[slot_u, q, 0:ROW_CHUNKS, :].astype(F32)
                    parts.append(u[0:8] * hlo + u[8:16] * hhi)
                if do_v:
                    v = buf[slot_v, q, ROW_CHUNKS:2 * ROW_CHUNKS, :].astype(F32)
                    accs[kk % 4] = accs[kk % 4] + c_scr[par, pl.ds(_coef_row(q), 1), :] * v
            if do_u:
                merged.append(_merge8(parts, sub))
        if do_v:
            rows = _row_slice(s)
            y_ref[rows, :] = x_ref[rows, :] + ((accs[0] + accs[1]) + (accs[2] + accs[3]))
        if do_u:
            for j in range(PEER_SEL // 8):
                m_scr[par, 8 * j:8 * j + 8, :] = merged[j]
        if do_c:
            c_scr[1 - par] = jnp.broadcast_to(coef, (PEER_SEL, LANES))

    @pl.when(step_i == 0)
    def _():
        for tok in range(MX_AHEAD):
            for q in range(PEER_SEL):
                row_copy(e_ref, tok, q, tok).start(priority=q % 2)

    step(-2, MX_SLOTS - 2, None, None, True, False, False)
    step(-1, MX_SLOTS - 1, None, None, True, True, False)

    def body(it, carry):
        for r in range(MX_SLOTS):
            s = it * MX_SLOTS + r
            step(s, r, e_ref, s + MX_AHEAD, True, True, True)
        return carry

    lax.fori_loop(0, MX_TB // MX_SLOTS - 1, body, 0)
    for s in range(MX_TB - MX_SLOTS, MX_TB):
        nxt = s + MX_AHEAD - MX_TB
        src, tok = (e_ref, s + MX_AHEAD) if nxt < 0 else (en_ref, nxt)
        step(s, s % MX_SLOTS, src, tok, s + 2 < MX_TB, s + 1 < MX_TB, True)

    @pl.when(step_i == pl.num_programs(0) - 1)
    def _():
        for tok in range(MX_AHEAD):
            wait_rows(tok)


def _peer_mix(x1, e_t, g_t, gffn, table):
    t_tok = x1.shape[0]
    assert MX_SLOTS & (MX_SLOTS - 1) == 0 and MX_TB % MX_SLOTS == 0 and MX_AHEAD <= 8
    rows = MX_TB * ROW_CHUNKS
    last8 = t_tok // 8 - 1
    y_rows = pl.pallas_call(
        _peer_mix_kernel,
        grid=(t_tok // MX_TB,),
        in_specs=[pl.BlockSpec((MX_TB, PEER_SEL), lambda i: (i, 0), memory_space=pltpu.SMEM),
                  pl.BlockSpec((8, PEER_SEL), lambda i: (jnp.minimum((i + 1) * (MX_TB // 8), last8), 0),
                               memory_space=pltpu.SMEM),
                  pl.BlockSpec((rows, LANES), lambda i: (i, 0)),
                  pl.BlockSpec((PEER_SEL, MX_TB), lambda i: (0, i)),
                  pl.BlockSpec((ROW_CHUNKS, 1, LANES), lambda i: (0, 0, 0)),
                  pl.BlockSpec(memory_space=pl.ANY)],
        out_specs=pl.BlockSpec((rows, LANES), lambda i: (i, 0)),
        out_shape=jax.ShapeDtypeStruct((t_tok * ROW_CHUNKS, LANES), F32),
        scratch_shapes=[pltpu.VMEM((MX_SLOTS, PEER_SEL, 2 * ROW_CHUNKS, LANES), BF16),
                        pltpu.SemaphoreType.DMA((MX_SLOTS,)),
                        pltpu.VMEM((rows, LANES), F32),
                        pltpu.VMEM((2, PEER_SEL, LANES), F32),
                        pltpu.VMEM((2, PEER_SEL, LANES), F32)],
        compiler_params=pltpu.CompilerParams(dimension_semantics=("arbitrary",),
                                             vmem_limit_bytes=VMEM_LIMIT),
        name="peer_mix",
    )(e_t, e_t, _to_rows(x1), g_t, gffn.reshape(ROW_CHUNKS, 1, LANES), table)
    return _from_rows(y_rows, t_tok)


def _prepare(g_mix, w_in, qn_a, kn_a, qn_b, kn_b, w_br_a, w_br_b, w_out, g_ffn, w_pq, sub_keys,
             u_emb, v_emb):
    return dict(
        gmix=g_mix.reshape(1, D_MODEL),
        w_qkv=w_in[:, :QKV_WIDTH].astype(BF16),
        w_g=w_in[:, QKV_WIDTH:].astype(BF16),
        gains=jnp.stack([qn_a, kn_a, qn_b, kn_b]),
        tables=_rope_tables(),
        w_a=w_br_a.astype(BF16), w_b=w_br_b.astype(BF16), w_o=w_out.astype(BF16),
        gffn=g_ffn.reshape(1, D_MODEL),
        w_pq=w_pq.astype(BF16),
        sk=sub_keys.reshape(2 * PEER_HEADS, PEER_NKEYS, LANES).astype(BF16),
        table=_pack_table(u_emb, v_emb),
    )


def _layer(x, p):
    n_batch, length, d = x.shape
    assert length == SEQ and d == D_MODEL
    x2 = x.reshape(n_batch * length, d)
    qkv, d16 = _in_proj(x2, p["gmix"], p["w_qkv"], p["gains"], p["tables"])
    ob = _attn_b(qkv, n_batch)
    o2, lse2 = _attn_g2(d16, n_batch)
    oa = _attn_a(qkv, o2, lse2, n_batch)
    x1 = _merge(x2, oa, ob, p["gmix"], p["w_g"], p["w_a"], p["w_b"], p["w_o"])
    e_t, g_t = _peer_route(x1, p["gffn"], p["w_pq"], p["sk"])
    y = _peer_mix(x1, e_t, g_t, p["gffn"], p["table"])
    return y.reshape(n_batch, length, d)


def kernel(x_prompt, x_sample, g_mix, w_in, qn_a, kn_a, qn_b, kn_b, w_br_a, w_br_b, w_out, g_ffn,
           w_pq, sub_keys, u_emb, v_emb):
    y_prompt, y_sample = x_prompt, x_sample
    for l in range(g_mix.shape[0]):
        p = _prepare(g_mix[l], w_in[l], qn_a[l], kn_a[l], qn_b[l], kn_b[l], w_br_a[l], w_br_b[l],
                     w_out[l], g_ffn[l], w_pq[l], sub_keys[l], u_emb[l], v_emb[l])
        y_prompt = _layer(y_prompt, p)
        y_sample = _layer(y_sample, p)
    return (y_prompt, y_sample)
```

```python
import functools
import math

import jax
import jax.numpy as jnp
from jax import lax
from jax.experimental import pallas as pl
from jax.experimental.pallas import tpu as pltpu

F32 = jnp.float32
BF16 = jnp.bfloat16

D_MODEL = 2048
SEQ = 2048
HEAD_DIM = 128
EPS = 1e-6
GRID_W = 64
SCALE = HEAD_DIM ** -0.5
A_HEADS_PER_GROUP = 4
A_DILATIONS = (1, 4, 16)
A_HALF = 64
ROPE_THETA_A = 500000.0
ROT_A = HEAD_DIM // 4
ROPE_THETA_B = 10000.0
AXIAL_HALF = HEAD_DIM // 2
QKV_WIDTH = 6144
QA0, KA0, VA0, QB0, KB0, VB0 = 0, 12, 24, 36, 44, 46
N_QKV_HEADS = 48
PEER_HEADS = 8
PEER_NKEYS = 128
PEER_TOPK = 16
PEER_SEL = PEER_HEADS * PEER_TOPK
ROW_CHUNKS = D_MODEL // 128
NEG = -1e30
INV_SQRT2 = 0.7071067811865476

LANES = 128
VMEM_LIMIT = 56 * 1024 * 1024


def _to_rows(x2):
    n = x2.shape[0]
    return x2.reshape(n // 8, 8, ROW_CHUNKS, LANES).transpose(0, 2, 1, 3).reshape(n * ROW_CHUNKS, LANES)


def _from_rows(r2, n):
    return r2.reshape(n // 8, ROW_CHUNKS, 8, LANES).transpose(0, 2, 1, 3).reshape(n, ROW_CHUNKS * LANES)


def _row_slice(i):
    return pl.ds((i >> 3) * (8 * ROW_CHUNKS) + (i & 7), ROW_CHUNKS, stride=8)


PACK_BLOCKS = 8


def _pack_table_kernel(u_ref, v_ref, o_ref):
    for e in range(8 * PACK_BLOCKS):
        o_ref[e, 0:ROW_CHUNKS, :] = u_ref[_row_slice(e), :].astype(BF16)
        o_ref[e, ROW_CHUNKS:2 * ROW_CHUNKS, :] = v_ref[_row_slice(e), :].astype(BF16)


def _pack_table(u_emb, v_emb):
    n = u_emb.shape[0]
    experts = 8 * PACK_BLOCKS
    rows = experts * ROW_CHUNKS
    return pl.pallas_call(
        _pack_table_kernel,
        grid=(n // experts,),
        in_specs=[pl.BlockSpec((rows, LANES), lambda i: (i, 0)),
                  pl.BlockSpec((rows, LANES), lambda i: (i, 0))],
        out_specs=pl.BlockSpec((experts, 2 * ROW_CHUNKS, LANES), lambda i: (i, 0, 0)),
        out_shape=jax.ShapeDtypeStruct((n, 2 * ROW_CHUNKS, LANES), BF16),
        name="pack_table",
    )(_to_rows(u_emb), _to_rows(v_emb))


def _rope_tables():
    pos = jnp.arange(SEQ, dtype=F32)
    ha = ROT_A // 2
    inv_a = ROPE_THETA_A ** (-(jnp.arange(0, ROT_A, 2, dtype=F32) / ROT_A))
    ang = pos[:, None] * inv_a[None, :]
    cos, sin = jnp.cos(ang), jnp.sin(ang)
    pad = jnp.zeros((SEQ, HEAD_DIM - ROT_A), F32)
    zh = jnp.zeros((SEQ, ha), F32)
    ca = jnp.concatenate([cos, cos, pad + 1.0], axis=1)
    s1a = jnp.concatenate([-sin, zh, pad], axis=1)
    s2a = jnp.concatenate([zh, sin, pad], axis=1)
    rows = SEQ // GRID_W
    row_ids = jnp.repeat(jnp.arange(rows), GRID_W).astype(F32)
    col_ids = jnp.tile(jnp.arange(GRID_W), rows).astype(F32)
    inv_b = ROPE_THETA_B ** (-(jnp.arange(0, AXIAL_HALF, 2, dtype=F32) / AXIAL_HALF))
    ar = row_ids[:, None] * inv_b[None, :]
    ac = col_ids[:, None] * inv_b[None, :]
    zq = jnp.zeros_like(ar)
    cb = jnp.concatenate([jnp.cos(ar), jnp.cos(ar), jnp.cos(ac), jnp.cos(ac)], axis=1)
    s1b = jnp.concatenate([-jnp.sin(ar), zq, -jnp.sin(ac), zq], axis=1)
    s2b = jnp.concatenate([zq, jnp.sin(ar), zq, jnp.sin(ac)], axis=1)
    return ca, s1a, s2a, cb, s1b, s2b


IN_TM = 256
IN_TN = 256
GAIN_QA, GAIN_KA, GAIN_QB, GAIN_KB = range(4)
HEAD_KINDS = ([(GAIN_QA, "a")] * 12 + [(GAIN_KA, "a")] * 12 + [None] * 12
              + [(GAIN_QB, "b")] * 8 + [(GAIN_KB, "b")] * 2 + [None] * 2)


DIL_G2 = A_DILATIONS[2]
G2_HEADS = tuple(base + 2 * A_HEADS_PER_GROUP + i for base in (QA0, KA0, VA0) for i in range(A_HEADS_PER_GROUP))
G2_STEPS = SEQ // DIL_G2


def _in_proj_kernel(x_ref, gmix_ref, w_ref, gain_ref, ca_ref, s1a_ref, s2a_ref,
                    cb_ref, s1b_ref, s2b_ref, o_ref, d_ref, y_scr):
    x = x_ref[...]
    ms = jnp.mean(x * x, axis=-1, keepdims=True)
    h = (x * lax.rsqrt(ms + EPS) * gmix_ref[...]).astype(BF16)
    rot = {"a": (ca_ref, s1a_ref, s2a_ref, ROT_A // 2), "b": (cb_ref, s1b_ref, s2b_ref, AXIAL_HALF // 2)}
    heads_per_dot = IN_TN // HEAD_DIM
    for j in range(QKV_WIDTH // IN_TN):
        t = jnp.dot(h, w_ref[:, j * IN_TN:(j + 1) * IN_TN], preferred_element_type=F32)
        for hh in range(heads_per_dot):
            head = j * heads_per_dot + hh
            y = t[:, hh * HEAD_DIM:(hh + 1) * HEAD_DIM]
            if HEAD_KINDS[head] is not None:
                gain, kind = HEAD_KINDS[head]
                c_ref, s1_ref, s2_ref, sh = rot[kind]
                y = y * lax.rsqrt(jnp.mean(y * y, axis=-1, keepdims=True) + EPS) * gain_ref[gain:gain + 1, :]
                y = (y * c_ref[...]
                     + pltpu.roll(y, HEAD_DIM - sh, 1) * s1_ref[...]
                     + pltpu.roll(y, sh, 1) * s2_ref[...])
            o_ref[head] = y.astype(BF16)
            if head in G2_HEADS:
                y_scr[...] = y
                for res in range(DIL_G2):
                    d_ref[G2_HEADS.index(head), 0, :, res * HEAD_DIM:(res + 1) * HEAD_DIM] = (
                        y_scr[pl.ds(res, IN_TM // DIL_G2, stride=DIL_G2), :].astype(BF16))


def _in_proj(x2, gmix, w_qkv, gains, tables):
    t_tok = x2.shape[0]
    nseq = SEQ // IN_TM
    tab_spec = pl.BlockSpec((IN_TM, HEAD_DIM), lambda i: (i % nseq, 0))
    return pl.pallas_call(
        _in_proj_kernel,
        grid=(t_tok // IN_TM,),
        in_specs=[pl.BlockSpec((IN_TM, D_MODEL), lambda i: (i, 0)),
                  _resident(gmix.shape), _resident(w_qkv.shape), _resident(gains.shape),
                  tab_spec, tab_spec, tab_spec, tab_spec, tab_spec, tab_spec],
        out_specs=[pl.BlockSpec((N_QKV_HEADS, IN_TM, HEAD_DIM), lambda i: (0, i, 0)),
                   pl.BlockSpec((len(G2_HEADS), 1, IN_TM // DIL_G2, DIL_G2 * HEAD_DIM),
                                lambda i: (0, i // nseq, i % nseq, 0))],
        out_shape=[jax.ShapeDtypeStruct((N_QKV_HEADS, t_tok, HEAD_DIM), BF16),
                   jax.ShapeDtypeStruct((len(G2_HEADS), t_tok // SEQ, G2_STEPS, DIL_G2 * HEAD_DIM), BF16)],
        scratch_shapes=[pltpu.VMEM((IN_TM, HEAD_DIM), F32)],
        compiler_params=pltpu.CompilerParams(dimension_semantics=("arbitrary",),
                                             vmem_limit_bytes=VMEM_LIMIT),
        name="in_proj",
    )(x2, gmix, w_qkv, gains, *tables)


ATT_TQ = 256
NT = (((1,), (1,)), ((), ()))


def _attn_b_kernel(q_ref, k_ref, v_ref, o_ref):
    k = k_ref[0]
    v = v_ref[0]
    for hh in range(4):
        s = lax.dot_general(q_ref[hh], k, NT, preferred_element_type=F32) * SCALE
        m = jnp.max(s, axis=-1, keepdims=True)
        p = jnp.exp(s - m)
        l = jnp.sum(p, axis=-1, keepdims=True)
        o = jnp.dot(p.astype(BF16), v, preferred_element_type=F32) / l
        o_ref[:, hh * HEAD_DIM:(hh + 1) * HEAD_DIM] = o.astype(BF16)


def _attn_b(qkv, n_batch):
    t_tok = qkv.shape[1]
    nq = SEQ // ATT_TQ
    return pl.pallas_call(
        _attn_b_kernel,
        grid=(n_batch, 2, nq),
        in_specs=[pl.BlockSpec((4, ATT_TQ, HEAD_DIM), lambda b, g, qi: (QB0 // 4 + g, b * nq + qi, 0)),
                  pl.BlockSpec((1, SEQ, HEAD_DIM), lambda b, g, qi: (KB0 + g, b, 0)),
                  pl.BlockSpec((1, SEQ, HEAD_DIM), lambda b, g, qi: (VB0 + g, b, 0))],
        out_specs=pl.BlockSpec((ATT_TQ, 4 * HEAD_DIM), lambda b, g, qi: (b * nq + qi, g)),
        out_shape=jax.ShapeDtypeStruct((t_tok, 8 * HEAD_DIM), BF16),
        compiler_params=pltpu.CompilerParams(
            dimension_semantics=("arbitrary", "arbitrary", "arbitrary"), vmem_limit_bytes=VMEM_LIMIT),
        name="attn_b",
    )(qkv, qkv, qkv)


def _key_window(dil):
    span = A_HALF * dil
    span = -(-span // LANES) * LANES
    return min(SEQ, ATT_TQ + 2 * span), span


def _attn_g2_kernel(q_ref, k_ref, v_ref, o_ref, lse_ref):
    i = lax.broadcasted_iota(jnp.int32, (G2_STEPS, G2_STEPS), 0)
    j = lax.broadcasted_iota(jnp.int32, (G2_STEPS, G2_STEPS), 1)
    band = jnp.abs(i - j) <= A_HALF
    for res in range(DIL_G2):
        cols = slice(res * HEAD_DIM, (res + 1) * HEAD_DIM)
        s = lax.dot_general(q_ref[0, 0, :, cols], k_ref[0, 0, :, cols], NT, preferred_element_type=F32) * SCALE
        s = jnp.where(band, s, NEG)
        m = jnp.max(s, axis=-1, keepdims=True)
        p = jnp.exp(s - m)
        l = jnp.sum(p, axis=-1, keepdims=True)
        o = jnp.dot(p.astype(BF16), v_ref[0, 0, :, cols], preferred_element_type=F32) / l
        rows = pl.ds(res, G2_STEPS, stride=DIL_G2)
        o_ref[rows, :] = o
        lse_ref[rows, :] = jnp.broadcast_to(m + jnp.log(l), (G2_STEPS, HEAD_DIM))


def _attn_g2(d16, n_batch):
    g4 = A_HEADS_PER_GROUP
    width = DIL_G2 * HEAD_DIM

    def spec(base):
        return pl.BlockSpec((1, 1, G2_STEPS, width), lambda b, hh: (base + hh, b, 0, 0))

    out_spec = pl.BlockSpec((SEQ, HEAD_DIM), lambda b, hh: (b, hh))
    out_shape = jax.ShapeDtypeStruct((n_batch * SEQ, g4 * HEAD_DIM), F32)
    return pl.pallas_call(
        _attn_g2_kernel,
        grid=(n_batch, g4),
        in_specs=[spec(0), spec(g4), spec(2 * g4)],
        out_specs=[out_spec, out_spec],
        out_shape=[out_shape, out_shape],
        compiler_params=pltpu.CompilerParams(dimension_semantics=("arbitrary", "arbitrary"),
                                             vmem_limit_bytes=VMEM_LIMIT),
        name="attn_g2",
    )(d16, d16, d16)


def _attn_a_kernel(q0_ref, q1_ref, k0_ref, k1_ref, v0_ref, v1_ref, o2_ref, lse2_ref, o_ref):
    q0pos = pl.program_id(2) * ATT_TQ
    scores, values = [], []
    for q_ref, k_ref, v_ref, dil in ((q0_ref, k0_ref, v0_ref, A_DILATIONS[0]),
                                     (q1_ref, k1_ref, v1_ref, A_DILATIONS[1])):
        width, span = _key_window(dil)
        if width < SEQ:
            start = pl.multiple_of(jnp.clip(q0pos - span, 0, SEQ - width), LANES)
            kk = k_ref[0, pl.ds(start, width), :]
            vv = v_ref[0, pl.ds(start, width), :]
        else:
            start = 0
            kk = k_ref[0]
            vv = v_ref[0]
        s = lax.dot_general(q_ref[0], kk, NT, preferred_element_type=F32) * SCALE
        rel = (lax.broadcasted_iota(jnp.int32, (ATT_TQ, width), 1)
               - lax.broadcasted_iota(jnp.int32, (ATT_TQ, width), 0)) + (start - q0pos)
        valid = jnp.abs(rel) <= A_HALF * dil
        if dil > 1:
            valid = jnp.logical_and(valid, (rel & (dil - 1)) == 0)
        scores.append(jnp.where(valid, s, NEG))
        values.append(vv)
    lse2 = lse2_ref[:, 0:1]
    m = functools.reduce(jnp.maximum, [jnp.max(s, axis=-1, keepdims=True) for s in scores] + [lse2])
    w2 = jnp.exp(lse2 - m)
    l = w2
    acc = o2_ref[...] * w2
    for s, vv in zip(scores, values):
        p = jnp.exp(s - m)
        l = l + jnp.sum(p, axis=-1, keepdims=True)
        acc = acc + jnp.dot(p.astype(BF16), vv, preferred_element_type=F32)
    o_ref[...] = (acc / l).astype(BF16)


def _attn_a(qkv, o2, lse2, n_batch):
    t_tok = qkv.shape[1]
    nq = SEQ // ATT_TQ
    g4 = A_HEADS_PER_GROUP

    def q_spec(g):
        return pl.BlockSpec((1, ATT_TQ, HEAD_DIM), lambda b, hh, qi: (QA0 + g4 * g + hh, b * nq + qi, 0))

    def kv_spec(base, g):
        return pl.BlockSpec((1, SEQ, HEAD_DIM), lambda b, hh, qi: (base + g4 * g + hh, b, 0))

    tile_spec = pl.BlockSpec((ATT_TQ, HEAD_DIM), lambda b, hh, qi: (b * nq + qi, hh))
    return pl.pallas_call(
        _attn_a_kernel,
        grid=(n_batch, g4, nq),
        in_specs=[q_spec(0), q_spec(1), kv_spec(KA0, 0), kv_spec(KA0, 1), kv_spec(VA0, 0), kv_spec(VA0, 1),
                  tile_spec, tile_spec],
        out_specs=tile_spec,
        out_shape=jax.ShapeDtypeStruct((t_tok, g4 * HEAD_DIM), BF16),
        compiler_params=pltpu.CompilerParams(
            dimension_semantics=("arbitrary", "arbitrary", "arbitrary"), vmem_limit_bytes=VMEM_LIMIT),
        name="attn_a",
    )(*([qkv] * 6), o2, lse2)


MG_TM = 256
MG_CH = 512


def _sigmoid(z):
    return 1.0 / (1.0 + jnp.exp(-z))


def _merge_kernel(x_ref, oa_ref, ob_ref, gmix_ref, wg_ref, wa_ref, wb_ref, wo_ref, o_ref, m_scr):
    x = x_ref[...]
    ms = jnp.mean(x * x, axis=-1, keepdims=True)
    h = (x * lax.rsqrt(ms + EPS) * gmix_ref[...]).astype(BF16)
    oa = oa_ref[...]
    ob = ob_ref[...]
    for c in range(D_MODEL // MG_CH):
        lo, hi = c * MG_CH, (c + 1) * MG_CH
        ga = jnp.dot(h, wg_ref[:, lo:hi], preferred_element_type=F32)
        gb = jnp.dot(h, wg_ref[:, D_MODEL + lo:D_MODEL + hi], preferred_element_type=F32)
        pa = jnp.dot(oa, wa_ref[:, lo:hi], preferred_element_type=F32)
        pb = jnp.dot(ob, wb_ref[:, lo:hi], preferred_element_type=F32)
        m_scr[:, lo:hi] = (_sigmoid(ga) * pa + _sigmoid(gb) * pb).astype(BF16)
    o_ref[...] = x + jnp.dot(m_scr[...], wo_ref[...], preferred_element_type=F32)


def _resident(shape):
    nd = len(shape)
    return pl.BlockSpec(shape, lambda i: (0,) * nd, pipeline_mode=pl.Buffered(1))


def _merge(x2, oa, ob, gmix, w_g, w_a, w_b, w_o):
    t_tok = x2.shape[0]
    return pl.pallas_call(
        _merge_kernel,
        grid=(t_tok // MG_TM,),
        in_specs=[pl.BlockSpec((MG_TM, D_MODEL), lambda i: (i, 0)),
                  pl.BlockSpec((MG_TM, oa.shape[1]), lambda i: (i, 0)),
                  pl.BlockSpec((MG_TM, ob.shape[1]), lambda i: (i, 0)),
                  _resident(gmix.shape), _resident(w_g.shape), _resident(w_a.shape),
                  _resident(w_b.shape), _resident(w_o.shape)],
        out_specs=pl.BlockSpec((MG_TM, D_MODEL), lambda i: (i, 0)),
        out_shape=jax.ShapeDtypeStruct((t_tok, D_MODEL), F32),
        scratch_shapes=[pltpu.VMEM((MG_TM, D_MODEL), BF16)],
        compiler_params=pltpu.CompilerParams(dimension_semantics=("arbitrary",),
                                             vmem_limit_bytes=VMEM_LIMIT),
        name="merge",
    )(x2, oa, ob, gmix, w_g, w_a, w_b, w_o)


RT_TM = 256


def _top16_rows(s, key_f, val_ref, idx_ref):
    for r in range(PEER_TOPK):
        m = jnp.max(s, axis=0, keepdims=True)
        idx = jnp.min(jnp.where(s == m, key_f, float(PEER_NKEYS)), axis=0, keepdims=True)
        val_ref[r:r + 1, :] = m
        idx_ref[r:r + 1, :] = idx
        s = jnp.where(key_f == idx, -jnp.inf, s)


PERM8 = (0, 4, 2, 6, 1, 5, 3, 7)


def _coef_row(q):
    return (q // 8) * 8 + PERM8[q % 8]


def _peer_route_kernel(x_ref, gffn_ref, wpq_ref, sk_ref, e_ref, g_ref,
                       q_scr, v1_scr, i1_scr, v2_scr, i2_scr, cv_scr, ce_scr, et_scr):
    x = x_ref[...]
    ms = jnp.mean(x * x, axis=-1, keepdims=True)
    h = (x * lax.rsqrt(ms + EPS) * gffn_ref[...]).astype(BF16)
    q = jnp.dot(h, wpq_ref[...], preferred_element_type=F32)
    for hc in range(2 * PEER_HEADS):
        q_scr[hc] = q[:, hc * LANES:(hc + 1) * LANES].astype(BF16)

    key_f = lax.broadcasted_iota(jnp.int32, (PEER_NKEYS, LANES), 0).astype(F32)
    sub = lax.broadcasted_iota(jnp.int32, (8, LANES), 0)
    zero8 = jnp.zeros_like(sub)
    ca = jnp.concatenate([zero8, zero8, zero8 + 1, zero8 + 2, 3 + (sub >> 2), 5 + (sub >> 1), 8 + sub], axis=0)
    cb = jnp.concatenate([sub, sub + 8, sub, sub, sub & 3, sub & 1, zero8], axis=0)
    pairs_ok = jnp.concatenate([zero8 == 0] * 5 + [sub < 6, zero8 == 0], axis=0)
    cand_ok = jnp.logical_and((ca + 1) * (cb + 1) <= PEER_TOPK, pairs_ok)
    flat_f = jnp.where(cand_ok, ca * PEER_TOPK + cb, -1).astype(F32)

    def cand_tiles(t1, t2, combine):
        lo2, hi2 = t2[0:8], t2[8:16]
        return jnp.concatenate([
            combine(t1[0:1], lo2), combine(t1[0:1], hi2), combine(t1[1:2], lo2), combine(t1[2:3], lo2),
            combine(jnp.where(sub < 4, t1[3:4], t1[4:5]), jnp.where(sub < 4, lo2, pltpu.roll(lo2, 4, 0))),
            combine(jnp.where(sub < 2, t1[5:6], jnp.where(sub < 4, t1[6:7], t1[7:8])),
                    jnp.where((sub & 1) == 0, t2[0:1], t2[1:2])),
            combine(t1[8:16], t2[0:1])], axis=0)

    def head_body(hd, carry):
        for lh in range(RT_TM // LANES):
            for c, (val_ref, idx_ref) in enumerate(((v1_scr, i1_scr), (v2_scr, i2_scr))):
                s = lax.dot_general(sk_ref[hd * 2 + c], q_scr[hd * 2 + c, pl.ds(lh * LANES, LANES), :],
                                    NT, preferred_element_type=F32)
                _top16_rows(s, key_f, val_ref, idx_ref)
            v1, i1 = v1_scr[...], i1_scr[...]
            v2, i2 = v2_scr[...], i2_scr[...]
            cand = jnp.where(cand_ok, cand_tiles(v1, v2, lambda x, y: x + y), -jnp.inf)
            cexp = cand_tiles(i1, i2, lambda x, y: x * float(PEER_NKEYS) + y)
            for r in range(PEER_TOPK):
                m = jnp.max(cand, axis=0, keepdims=True)
                sel = jnp.min(jnp.where(cand == m, flat_f, 1e9), axis=0, keepdims=True)
                hit = flat_f == sel
                cv_scr[_coef_row(r):_coef_row(r) + 1, :] = m
                ce_scr[r:r + 1, :] = jnp.max(jnp.where(hit, cexp, -1.0), axis=0, keepdims=True)
                cand = jnp.where(hit, -jnp.inf, cand)
            cv = cv_scr[...]
            w = jnp.exp(cv - cv[0:1])
            w = w / jnp.sum(w, axis=0, keepdims=True)
            rows = pl.ds(pl.multiple_of(hd * PEER_TOPK, PEER_TOPK), PEER_TOPK)
            et_scr[rows, lh * LANES:(lh + 1) * LANES] = ce_scr[...]
            g_ref[rows, lh * LANES:(lh + 1) * LANES] = w
        return carry

    lax.fori_loop(0, PEER_HEADS, head_body, 0)
    e_ref[...] = et_scr[...].T.astype(jnp.int32)


def _peer_route(x1, gffn, w_pq, sk):
    t_tok = x1.shape[0]
    small = [pltpu.VMEM((PEER_TOPK, LANES), F32) for _ in range(6)]
    return pl.pallas_call(
        _peer_route_kernel,
        grid=(t_tok // RT_TM,),
        in_specs=[pl.BlockSpec((RT_TM, D_MODEL), lambda i: (i, 0)),
                  _resident(gffn.shape), _resident(w_pq.shape), _resident(sk.shape)],
        out_specs=[pl.BlockSpec((RT_TM, PEER_SEL), lambda i: (i, 0)),
                   pl.BlockSpec((PEER_SEL, RT_TM), lambda i: (0, i))],
        out_shape=[jax.ShapeDtypeStruct((t_tok, PEER_SEL), jnp.int32),
                   jax.ShapeDtypeStruct((PEER_SEL, t_tok), F32)],
        scratch_shapes=[pltpu.VMEM((2 * PEER_HEADS, RT_TM, LANES), BF16)] + small
                       + [pltpu.VMEM((PEER_SEL, RT_TM), F32)],
        compiler_params=pltpu.CompilerParams(dimension_semantics=("arbitrary",),
                                             vmem_limit_bytes=VMEM_LIMIT),
        name="peer_route",
    )(x1, gffn, w_pq, sk)


MX_TB = 128
MX_SLOTS = 8
MX_AHEAD = 7


def _merge8(ps, sub):
    lo4, lo2, lo1 = (sub & 4) == 0, (sub & 2) == 0, (sub & 1) == 0
    q = [jnp.where(lo4, ps[2 * i], ps[2 * i + 1])
         + pltpu.roll(jnp.where(lo4, ps[2 * i + 1], ps[2 * i]), 4, 0) for i in range(4)]
    r = [jnp.where(lo2, q[2 * i] + pltpu.roll(q[2 * i], 6, 0), q[2 * i + 1] + pltpu.roll(q[2 * i + 1], 2, 0))
         for i in range(2)]
    return jnp.where(lo1, r[0] + pltpu.roll(r[0], 7, 0), r[1] + pltpu.roll(r[1], 1, 0))


def _peer_mix_kernel(e_ref, en_ref, x_ref, g_ref, gffn_ref, tab_ref, y_ref, buf, sem, h_scr, m_scr,
                     c_scr):
    step_i = pl.program_id(0)
    x = x_ref[...].reshape(MX_TB // 8, ROW_CHUNKS, 8, LANES)
    ss = jnp.sum(jnp.sum(x * x, axis=3, keepdims=True), axis=1, keepdims=True)
    h = x * lax.rsqrt(ss * (1.0 / D_MODEL) + EPS) * gffn_ref[...]
    h_scr[...] = h.reshape(MX_TB * ROW_CHUNKS, LANES)

    sub = lax.broadcasted_iota(jnp.int32, (8, LANES), 0)
    lane = lax.broadcasted_iota(jnp.int32, (PEER_SEL, MX_TB), 1)

    def row_copy(idx_ref, tok, k, slot):
        return pltpu.make_async_copy(tab_ref.at[idx_ref[tok, k]], buf.at[slot, k], sem.at[slot])

    def wait_rows(slot):
        pltpu.make_async_copy(tab_ref.at[pl.ds(0, PEER_SEL)], buf.at[slot], sem.at[slot]).wait()

    def step(s, r, idx_ref, itok, do_u, do_c, do_v):
        slot_v, slot_u, slot_i = r % MX_SLOTS, (r + 2) % MX_SLOTS, (r + MX_AHEAD) % MX_SLOTS
        par = r % 2
        if do_u:
            wait_rows(slot_u)
            hrow = h_scr[_row_slice(s + 2), :]
            hlo, hhi = hrow[0:8], hrow[8:16]
        if do_c:
            a = jnp.sum(m_scr[1 - par], axis=1, keepdims=True)
            gate = jnp.sum(jnp.where(lane == s + 1, g_ref[...], 0.0), axis=1, keepdims=True)
            coef = 0.5 * a * (1.0 + lax.erf(a * INV_SQRT2)) * gate
        zero = jnp.zeros((ROW_CHUNKS, LANES), F32)
        accs = [zero, zero, zero, zero]
        merged = []
        if idx_ref is not None:
            slot_i = slot_i + (idx_ref[itok, 0] >> 31)
        for j in range(PEER_SEL // 8):
            parts = []
            for kk in range(8):
                q = 8 * j + kk
                if idx_ref is not None:
                    row_copy(idx_ref, itok, q, slot_i).start(priority=q % 2)
                if do_u:
                    u = buf[slot_u, q, 0:ROW_CHUNKS, :].astype(F32)
                    parts.append(u[0:8] * hlo + u[8:16] * hhi)
                if do_v:
                    v = buf[slot_v, q, ROW_CHUNKS:2 * ROW_CHUNKS, :].astype(F32)
                    accs[kk % 4] = accs[kk % 4] + c_scr[par, pl.ds(_coef_row(q), 1), :] * v
            if do_u:
                merged.append(_merge8(parts, sub))
        if do_v:
            rows = _row_slice(s)
            y_ref[rows, :] = x_ref[rows, :] + ((accs[0] + accs[1]) + (accs[2] + accs[3]))
        if do_u:
            for j in range(PEER_SEL // 8):
                m_scr[par, 8 * j:8 * j + 8, :] = merged[j]
        if do_c:
            c_scr[1 - par] = jnp.broadcast_to(coef, (PEER_SEL, LANES))

    @pl.when(step_i == 0)
    def _():
        for tok in range(MX_AHEAD):
            for q in range(PEER_SEL):
                row_copy(e_ref, tok, q, tok).start(priority=q % 2)

    step(-2, MX_SLOTS - 2, None, None, True, False, False)
    step(-1, MX_SLOTS - 1, None, None, True, True, False)

    def body(it, carry):
        for r in range(MX_SLOTS):
            s = it * MX_SLOTS + r
            step(s, r, e_ref, s + MX_AHEAD, True, True, True)
        return carry

    lax.fori_loop(0, MX_TB // MX_SLOTS - 1, body, 0)
    for s in range(MX_TB - MX_SLOTS, MX_TB):
        nxt = s + MX_AHEAD - MX_TB
        src, tok = (e_ref, s + MX_AHEAD) if nxt < 0 else (en_ref, nxt)
        step(s, s % MX_SLOTS, src, tok, s + 2 < MX_TB, s + 1 < MX_TB, True)

    @pl.when(step_i == pl.num_programs(0) - 1)
    def _():
        for tok in range(MX_AHEAD):
            wait_rows(tok)


def _peer_mix(x1, e_t, g_t, gffn, table):
    t_tok = x1.shape[0]
    assert MX_SLOTS & (MX_SLOTS - 1) == 0 and MX_TB % MX_SLOTS == 0 and MX_AHEAD <= 8
    rows = MX_TB * ROW_CHUNKS
    last8 = t_tok // 8 - 1
    y_rows = pl.pallas_call(
        _peer_mix_kernel,
        grid=(t_tok // MX_TB,),
        in_specs=[pl.BlockSpec((MX_TB, PEER_SEL), lambda i: (i, 0), memory_space=pltpu.SMEM),
                  pl.BlockSpec((8, PEER_SEL), lambda i: (jnp.minimum((i + 1) * (MX_TB // 8), last8), 0),
                               memory_space=pltpu.SMEM),
                  pl.BlockSpec((rows, LANES), lambda i: (i, 0)),
                  pl.BlockSpec((PEER_SEL, MX_TB), lambda i: (0, i)),
                  pl.BlockSpec((ROW_CHUNKS, 1, LANES), lambda i: (0, 0, 0)),
                  pl.BlockSpec(memory_space=pl.ANY)],
        out_specs=pl.BlockSpec((rows, LANES), lambda i: (i, 0)),
        out_shape=jax.ShapeDtypeStruct((t_tok * ROW_CHUNKS, LANES), F32),
        scratch_shapes=[pltpu.VMEM((MX_SLOTS, PEER_SEL, 2 * ROW_CHUNKS, LANES), BF16),
                        pltpu.SemaphoreType.DMA((MX_SLOTS,)),
                        pltpu.VMEM((rows, LANES), F32),
                        pltpu.VMEM((2, PEER_SEL, LANES), F32),
                        pltpu.VMEM((2, PEER_SEL, LANES), F32)],
        compiler_params=pltpu.CompilerParams(dimension_semantics=("arbitrary",),
                                             vmem_limit_bytes=VMEM_LIMIT),
        name="peer_mix",
    )(e_t, e_t, _to_rows(x1), g_t, gffn.reshape(ROW_CHUNKS, 1, LANES), table)
    return _from_rows(y_rows, t_tok)


def _prepare(g_mix, w_in, qn_a, kn_a, qn_b, kn_b, w_br_a, w_br_b, w_out, g_ffn, w_pq, sub_keys,
             u_emb, v_emb):
    return dict(
        gmix=g_mix.reshape(1, D_MODEL),
        w_qkv=w_in[:, :QKV_WIDTH].astype(BF16),
        w_g=w_in[:, QKV_WIDTH:].astype(BF16),
        gains=jnp.stack([qn_a, kn_a, qn_b, kn_b]),
        tables=_rope_tables(),
        w_a=w_br_a.astype(BF16), w_b=w_br_b.astype(BF16), w_o=w_out.astype(BF16),
        gffn=g_ffn.reshape(1, D_MODEL),
        w_pq=w_pq.astype(BF16),
        sk=sub_keys.reshape(2 * PEER_HEADS, PEER_NKEYS, LANES).astype(BF16),
        table=_pack_table(u_emb, v_emb),
    )


def _layer(x, p):
    n_batch, length, d = x.shape
    assert length == SEQ and d == D_MODEL
    x2 = x.reshape(n_batch * length, d)
    qkv, d16 = _in_proj(x2, p["gmix"], p["w_qkv"], p["gains"], p["tables"])
    ob = _attn_b(qkv, n_batch)
    o2, lse2 = _attn_g2(d16, n_batch)
    oa = _attn_a(qkv, o2, lse2, n_batch)
    x1 = _merge(x2, oa, ob, p["gmix"], p["w_g"], p["w_a"], p["w_b"], p["w_o"])
    e_t, g_t = _peer_route(x1, p["gffn"], p["w_pq"], p["sk"])
    y = _peer_mix(x1, e_t, g_t, p["gffn"], p["table"])
    return y.reshape(n_batch, length, d)


def kernel(x_prompt, x_sample, g_mix, w_in, qn_a, kn_a, qn_b, kn_b, w_br_a, w_br_b, w_out, g_ffn,
           w_pq, sub_keys, u_emb, v_emb):
    y_prompt, y_sample = x_prompt, x_sample
    for l in range(g_mix.shape[0]):
        p = _prepare(g_mix[l], w_in[l], qn_a[l], kn_a[l], qn_b[l], kn_b[l], w_br_a[l], w_br_b[l],
                     w_out[l], g_ffn[l], w_pq[l], sub_keys[l], u_emb[l], v_emb[l])
        y_prompt = _layer(y_prompt, p)
        y_sample = _layer(y_sample, p)
    return (y_prompt, y_sample)
```

```python
import functools
import math

import jax
import jax.numpy as jnp
from jax import lax
from jax.experimental import pallas as pl
from jax.experimental.pallas import tpu as pltpu

F32 = jnp.float32
BF16 = jnp.bfloat16

D_MODEL = 2048
SEQ = 2048
HEAD_DIM = 128
EPS = 1e-6
GRID_W = 64
SCALE = HEAD_DIM ** -0.5
A_HEADS_PER_GROUP = 4
A_DILATIONS = (1, 4, 16)
A_HALF = 64
ROPE_THETA_A = 500000.0
ROT_A = HEAD_DIM // 4
ROPE_THETA_B = 10000.0
AXIAL_HALF = HEAD_DIM // 2
QKV_WIDTH = 6144
QA0, KA0, VA0, QB0, KB0, VB0 = 0, 12, 24, 36, 44, 46
N_QKV_HEADS = 48
PEER_HEADS = 8
PEER_NKEYS = 128
PEER_TOPK = 16
PEER_SEL = PEER_HEADS * PEER_TOPK
ROW_CHUNKS = D_MODEL // 128
NEG = -1e30
INV_SQRT2 = 0.7071067811865476

LANES = 128
VMEM_LIMIT = 56 * 1024 * 1024


def _to_rows(x2):
    n = x2.shape[0]
    return x2.reshape(n // 8, 8, ROW_CHUNKS, LANES).transpose(0, 2, 1, 3).reshape(n * ROW_CHUNKS, LANES)


def _from_rows(r2, n):
    return r2.reshape(n // 8, ROW_CHUNKS, 8, LANES).transpose(0, 2, 1, 3).reshape(n, ROW_CHUNKS * LANES)


def _row_slice(i):
    return pl.ds((i >> 3) * (8 * ROW_CHUNKS) + (i & 7), ROW_CHUNKS, stride=8)


PACK_BLOCKS = 32


def _pack_table_kernel(u_ref, v_ref, o_ref):
    for e in range(8 * PACK_BLOCKS):
        o_ref[e, 0:ROW_CHUNKS, :] = u_ref[_row_slice(e), :].astype(BF16)
        o_ref[e, ROW_CHUNKS:2 * ROW_CHUNKS, :] = v_ref[_row_slice(e), :].astype(BF16)


def _pack_table(u_emb, v_emb):
    n = u_emb.shape[0]
    experts = 8 * PACK_BLOCKS
    rows = experts * ROW_CHUNKS
    return pl.pallas_call(
        _pack_table_kernel,
        grid=(n // experts,),
        in_specs=[pl.BlockSpec((rows, LANES), lambda i: (i, 0)),
                  pl.BlockSpec((rows, LANES), lambda i: (i, 0))],
        out_specs=pl.BlockSpec((experts, 2 * ROW_CHUNKS, LANES), lambda i: (i, 0, 0)),
        out_shape=jax.ShapeDtypeStruct((n, 2 * ROW_CHUNKS, LANES), BF16),
        name="pack_table",
    )(_to_rows(u_emb), _to_rows(v_emb))


def _rope_tables():
    pos = jnp.arange(SEQ, dtype=F32)
    ha = ROT_A // 2
    inv_a = ROPE_THETA_A ** (-(jnp.arange(0, ROT_A, 2, dtype=F32) / ROT_A))
    ang = pos[:, None] * inv_a[None, :]
    cos, sin = jnp.cos(ang), jnp.sin(ang)
    pad = jnp.zeros((SEQ, HEAD_DIM - ROT_A), F32)
    zh = jnp.zeros((SEQ, ha), F32)
    ca = jnp.concatenate([cos, cos, pad + 1.0], axis=1)
    s1a = jnp.concatenate([-sin, zh, pad], axis=1)
    s2a = jnp.concatenate([zh, sin, pad], axis=1)
    rows = SEQ // GRID_W
    row_ids = jnp.repeat(jnp.arange(rows), GRID_W).astype(F32)
    col_ids = jnp.tile(jnp.arange(GRID_W), rows).astype(F32)
    inv_b = ROPE_THETA_B ** (-(jnp.arange(0, AXIAL_HALF, 2, dtype=F32) / AXIAL_HALF))
    ar = row_ids[:, None] * inv_b[None, :]
    ac = col_ids[:, None] * inv_b[None, :]
    zq = jnp.zeros_like(ar)
    cb = jnp.concatenate([jnp.cos(ar), jnp.cos(ar), jnp.cos(ac), jnp.cos(ac)], axis=1)
    s1b = jnp.concatenate([-jnp.sin(ar), zq, -jnp.sin(ac), zq], axis=1)
    s2b = jnp.concatenate([zq, jnp.sin(ar), zq, jnp.sin(ac)], axis=1)
    return ca, s1a, s2a, cb, s1b, s2b


IN_TM = 256
IN_TN = 256
GAIN_QA, GAIN_KA, GAIN_QB, GAIN_KB = range(4)
HEAD_KINDS = ([(GAIN_QA, "a")] * 12 + [(GAIN_KA, "a")] * 12 + [None] * 12
              + [(GAIN_QB, "b")] * 8 + [(GAIN_KB, "b")] * 2 + [None] * 2)


DIL_G2 = A_DILATIONS[2]
G2_HEADS = tuple(base + 2 * A_HEADS_PER_GROUP + i for base in (QA0, KA0, VA0) for i in range(A_HEADS_PER_GROUP))
G2_STEPS = SEQ // DIL_G2


def _in_proj_kernel(x_ref, gmix_ref, w_ref, gain_ref, ca_ref, s1a_ref, s2a_ref,
                    cb_ref, s1b_ref, s2b_ref, o_ref, d_ref, y_scr):
    x = x_ref[...]
    ms = jnp.mean(x * x, axis=-1, keepdims=True)
    h = (x * lax.rsqrt(ms + EPS) * gmix_ref[...]).astype(BF16)
    rot = {"a": (ca_ref, s1a_ref, s2a_ref, ROT_A // 2), "b": (cb_ref, s1b_ref, s2b_ref, AXIAL_HALF // 2)}
    heads_per_dot = IN_TN // HEAD_DIM
    for j in range(QKV_WIDTH // IN_TN):
        t = jnp.dot(h, w_ref[:, j * IN_TN:(j + 1) * IN_TN], preferred_element_type=F32)
        for hh in range(heads_per_dot):
            head = j * heads_per_dot + hh
            y = t[:, hh * HEAD_DIM:(hh + 1) * HEAD_DIM]
            if HEAD_KINDS[head] is not None:
                gain, kind = HEAD_KINDS[head]
                c_ref, s1_ref, s2_ref, sh = rot[kind]
                y = y * lax.rsqrt(jnp.mean(y * y, axis=-1, keepdims=True) + EPS) * gain_ref[gain:gain + 1, :]
                y = (y * c_ref[...]
                     + pltpu.roll(y, HEAD_DIM - sh, 1) * s1_ref[...]
                     + pltpu.roll(y, sh, 1) * s2_ref[...])
            o_ref[head] = y.astype(BF16)
            if head in G2_HEADS:
                y_scr[...] = y
                for res in range(DIL_G2):
                    d_ref[G2_HEADS.index(head), 0, :, res * HEAD_DIM:(res + 1) * HEAD_DIM] = (
                        y_scr[pl.ds(res, IN_TM // DIL_G2, stride=DIL_G2), :].astype(BF16))


def _in_proj(x2, gmix, w_qkv, gains, tables):
    t_tok = x2.shape[0]
    nseq = SEQ // IN_TM
    tab_spec = pl.BlockSpec((IN_TM, HEAD_DIM), lambda i: (i % nseq, 0))
    return pl.pallas_call(
        _in_proj_kernel,
        grid=(t_tok // IN_TM,),
        in_specs=[pl.BlockSpec((IN_TM, D_MODEL), lambda i: (i, 0)),
                  _resident(gmix.shape), _resident(w_qkv.shape), _resident(gains.shape),
                  tab_spec, tab_spec, tab_spec, tab_spec, tab_spec, tab_spec],
        out_specs=[pl.BlockSpec((N_QKV_HEADS, IN_TM, HEAD_DIM), lambda i: (0, i, 0)),
                   pl.BlockSpec((len(G2_HEADS), 1, IN_TM // DIL_G2, DIL_G2 * HEAD_DIM),
                                lambda i: (0, i // nseq, i % nseq, 0))],
        out_shape=[jax.ShapeDtypeStruct((N_QKV_HEADS, t_tok, HEAD_DIM), BF16),
                   jax.ShapeDtypeStruct((len(G2_HEADS), t_tok // SEQ, G2_STEPS, DIL_G2 * HEAD_DIM), BF16)],
        scratch_shapes=[pltpu.VMEM((IN_TM, HEAD_DIM), F32)],
        compiler_params=pltpu.CompilerParams(dimension_semantics=("arbitrary",),
                                             vmem_limit_bytes=VMEM_LIMIT),
        name="in_proj",
    )(x2, gmix, w_qkv, gains, *tables)


ATT_TQ = 256
NT = (((1,), (1,)), ((), ()))


def _attn_b_kernel(q_ref, k_ref, v_ref, o_ref):
    k = k_ref[0]
    v = v_ref[0]
    for hh in range(4):
        s = lax.dot_general(q_ref[hh], k, NT, preferred_element_type=F32) * SCALE
        m = jnp.max(s, axis=-1, keepdims=True)
        p = jnp.exp(s - m)
        l = jnp.sum(p, axis=-1, keepdims=True)
        o = jnp.dot(p.astype(BF16), v, preferred_element_type=F32) / l
        o_ref[:, hh * HEAD_DIM:(hh + 1) * HEAD_DIM] = o.astype(BF16)


def _attn_b(qkv, n_batch):
    t_tok = qkv.shape[1]
    nq = SEQ // ATT_TQ
    return pl.pallas_call(
        _attn_b_kernel,
        grid=(n_batch, 2, nq),
        in_specs=[pl.BlockSpec((4, ATT_TQ, HEAD_DIM), lambda b, g, qi: (QB0 // 4 + g, b * nq + qi, 0)),
                  pl.BlockSpec((1, SEQ, HEAD_DIM), lambda b, g, qi: (KB0 + g, b, 0)),
                  pl.BlockSpec((1, SEQ, HEAD_DIM), lambda b, g, qi: (VB0 + g, b, 0))],
        out_specs=pl.BlockSpec((ATT_TQ, 4 * HEAD_DIM), lambda b, g, qi: (b * nq + qi, g)),
        out_shape=jax.ShapeDtypeStruct((t_tok, 8 * HEAD_DIM), BF16),
        compiler_params=pltpu.CompilerParams(
            dimension_semantics=("arbitrary", "arbitrary", "arbitrary"), vmem_limit_bytes=VMEM_LIMIT),
        name="attn_b",
    )(qkv, qkv, qkv)


def _key_window(dil):
    span = A_HALF * dil
    span = -(-span // LANES) * LANES
    return min(SEQ, ATT_TQ + 2 * span), span


def _attn_g2_kernel(q_ref, k_ref, v_ref, o_ref, lse_ref):
    i = lax.broadcasted_iota(jnp.int32, (G2_STEPS, G2_STEPS), 0)
    j = lax.broadcasted_iota(jnp.int32, (G2_STEPS, G2_STEPS), 1)
    band = jnp.abs(i - j) <= A_HALF
    for res in range(DIL_G2):
        cols = slice(res * HEAD_DIM, (res + 1) * HEAD_DIM)
        s = lax.dot_general(q_ref[0, 0, :, cols], k_ref[0, 0, :, cols], NT, preferred_element_type=F32) * SCALE
        s = jnp.where(band, s, NEG)
        m = jnp.max(s, axis=-1, keepdims=True)
        p = jnp.exp(s - m)
        l = jnp.sum(p, axis=-1, keepdims=True)
        o = jnp.dot(p.astype(BF16), v_ref[0, 0, :, cols], preferred_element_type=F32) / l
        rows = pl.ds(res, G2_STEPS, stride=DIL_G2)
        o_ref[rows, :] = o
        lse_ref[rows, :] = jnp.broadcast_to(m + jnp.log(l), (G2_STEPS, HEAD_DIM))


def _attn_g2(d16, n_batch):
    g4 = A_HEADS_PER_GROUP
    width = DIL_G2 * HEAD_DIM

    def spec(base):
        return pl.BlockSpec((1, 1, G2_STEPS, width), lambda b, hh: (base + hh, b, 0, 0))

    out_spec = pl.BlockSpec((SEQ, HEAD_DIM), lambda b, hh: (b, hh))
    out_shape = jax.ShapeDtypeStruct((n_batch * SEQ, g4 * HEAD_DIM), F32)
    return pl.pallas_call(
        _attn_g2_kernel,
        grid=(n_batch, g4),
        in_specs=[spec(0), spec(g4), spec(2 * g4)],
        out_specs=[out_spec, out_spec],
        out_shape=[out_shape, out_shape],
        compiler_params=pltpu.CompilerParams(dimension_semantics=("arbitrary", "arbitrary"),
                                             vmem_limit_bytes=VMEM_LIMIT),
        name="attn_g2",
    )(d16, d16, d16)


def _attn_a_kernel(q0_ref, q1_ref, k0_ref, k1_ref, v0_ref, v1_ref, o2_ref, lse2_ref, o_ref):
    q0pos = pl.program_id(2) * ATT_TQ
    scores, values = [], []
    for q_ref, k_ref, v_ref, dil in ((q0_ref, k0_ref, v0_ref, A_DILATIONS[0]),
                                     (q1_ref, k1_ref, v1_ref, A_DILATIONS[1])):
        width, span = _key_window(dil)
        if width < SEQ:
            start = pl.multiple_of(jnp.clip(q0pos - span, 0, SEQ - width), LANES)
            kk = k_ref[0, pl.ds(start, width), :]
            vv = v_ref[0, pl.ds(start, width), :]
        else:
            start = 0
            kk = k_ref[0]
            vv = v_ref[0]
        s = lax.dot_general(q_ref[0], kk, NT, preferred_element_type=F32) * SCALE
        rel = (lax.broadcasted_iota(jnp.int32, (ATT_TQ, width), 1)
               - lax.broadcasted_iota(jnp.int32, (ATT_TQ, width), 0)) + (start - q0pos)
        valid = jnp.abs(rel) <= A_HALF * dil
        if dil > 1:
            valid = jnp.logical_and(valid, (rel & (dil - 1)) == 0)
        scores.append(jnp.where(valid, s, NEG))
        values.append(vv)
    lse2 = lse2_ref[:, 0:1]
    m = functools.reduce(jnp.maximum, [jnp.max(s, axis=-1, keepdims=True) for s in scores] + [lse2])
    w2 = jnp.exp(lse2 - m)
    l = w2
    acc = o2_ref[...] * w2
    for s, vv in zip(scores, values):
        p = jnp.exp(s - m)
        l = l + jnp.sum(p, axis=-1, keepdims=True)
        acc = acc + jnp.dot(p.astype(BF16), vv, preferred_element_type=F32)
    o_ref[...] = (acc / l).astype(BF16)


def _attn_a(qkv, o2, lse2, n_batch):
    t_tok = qkv.shape[1]
    nq = SEQ // ATT_TQ
    g4 = A_HEADS_PER_GROUP

    def q_spec(g):
        return pl.BlockSpec((1, ATT_TQ, HEAD_DIM), lambda b, hh, qi: (QA0 + g4 * g + hh, b * nq + qi, 0))

    def kv_spec(base, g):
        return pl.BlockSpec((1, SEQ, HEAD_DIM), lambda b, hh, qi: (base + g4 * g + hh, b, 0))

    tile_spec = pl.BlockSpec((ATT_TQ, HEAD_DIM), lambda b, hh, qi: (b * nq + qi, hh))
    return pl.pallas_call(
        _attn_a_kernel,
        grid=(n_batch, g4, nq),
        in_specs=[q_spec(0), q_spec(1), kv_spec(KA0, 0), kv_spec(KA0, 1), kv_spec(VA0, 0), kv_spec(VA0, 1),
                  tile_spec, tile_spec],
        out_specs=tile_spec,
        out_shape=jax.ShapeDtypeStruct((t_tok, g4 * HEAD_DIM), BF16),
        compiler_params=pltpu.CompilerParams(
            dimension_semantics=("arbitrary", "arbitrary", "arbitrary"), vmem_limit_bytes=VMEM_LIMIT),
        name="attn_a",
    )(*([qkv] * 6), o2, lse2)


MG_TM = 256
MG_CH = 512


def _sigmoid(z):
    return 1.0 / (1.0 + jnp.exp(-z))


def _merge_kernel(x_ref, oa_ref, ob_ref, gmix_ref, wg_ref, wa_ref, wb_ref, wo_ref, o_ref, m_scr):
    x = x_ref[...]
    ms = jnp.mean(x * x, axis=-1, keepdims=True)
    h = (x * lax.rsqrt(ms + EPS) * gmix_ref[...]).astype(BF16)
    oa = oa_ref[...]
    ob = ob_ref[...]
    for c in range(D_MODEL // MG_CH):
        lo, hi = c * MG_CH, (c + 1) * MG_CH
        ga = jnp.dot(h, wg_ref[:, lo:hi], preferred_element_type=F32)
        gb = jnp.dot(h, wg_ref[:, D_MODEL + lo:D_MODEL + hi], preferred_element_type=F32)
        pa = jnp.dot(oa, wa_ref[:, lo:hi], preferred_element_type=F32)
        pb = jnp.dot(ob, wb_ref[:, lo:hi], preferred_element_type=F32)
        m_scr[:, lo:hi] = (_sigmoid(ga) * pa + _sigmoid(gb) * pb).astype(BF16)
    o_ref[...] = x + jnp.dot(m_scr[...], wo_ref[...], preferred_element_type=F32)


def _resident(shape):
    nd = len(shape)
    return pl.BlockSpec(shape, lambda i: (0,) * nd, pipeline_mode=pl.Buffered(1))


def _merge(x2, oa, ob, gmix, w_g, w_a, w_b, w_o):
    t_tok = x2.shape[0]
    return pl.pallas_call(
        _merge_kernel,
        grid=(t_tok // MG_TM,),
        in_specs=[pl.BlockSpec((MG_TM, D_MODEL), lambda i: (i, 0)),
                  pl.BlockSpec((MG_TM, oa.shape[1]), lambda i: (i, 0)),
                  pl.BlockSpec((MG_TM, ob.shape[1]), lambda i: (i, 0)),
                  _resident(gmix.shape), _resident(w_g.shape), _resident(w_a.shape),
                  _resident(w_b.shape), _resident(w_o.shape)],
        out_specs=pl.BlockSpec((MG_TM, D_MODEL), lambda i: (i, 0)),
        out_shape=jax.ShapeDtypeStruct((t_tok, D_MODEL), F32),
        scratch_shapes=[pltpu.VMEM((MG_TM, D_MODEL), BF16)],
        compiler_params=pltpu.CompilerParams(dimension_semantics=("arbitrary",),
                                             vmem_limit_bytes=VMEM_LIMIT),
        name="merge",
    )(x2, oa, ob, gmix, w_g, w_a, w_b, w_o)


RT_TM = 256


def _top16_rows(s, key_f, val_ref, idx_ref):
    for r in range(PEER_TOPK):
        m = jnp.max(s, axis=0, keepdims=True)
        idx = jnp.min(jnp.where(s == m, key_f, float(PEER_NKEYS)), axis=0, keepdims=True)
        val_ref[r:r + 1, :] = m
        idx_ref[r:r + 1, :] = idx
        s = jnp.where(key_f == idx, -jnp.inf, s)


PERM8 = (0, 4, 2, 6, 1, 5, 3, 7)


def _coef_row(q):
    return (q // 8) * 8 + PERM8[q % 8]


def _peer_route_kernel(x_ref, gffn_ref, wpq_ref, sk_ref, e_ref, g_ref,
                       q_scr, v1_scr, i1_scr, v2_scr, i2_scr, cv_scr, ce_scr, et_scr):
    x = x_ref[...]
    ms = jnp.mean(x * x, axis=-1, keepdims=True)
    h = (x * lax.rsqrt(ms + EPS) * gffn_ref[...]).astype(BF16)
    q = jnp.dot(h, wpq_ref[...], preferred_element_type=F32)
    for hc in range(2 * PEER_HEADS):
        q_scr[hc] = q[:, hc * LANES:(hc + 1) * LANES].astype(BF16)

    key_f = lax.broadcasted_iota(jnp.int32, (PEER_NKEYS, LANES), 0).astype(F32)
    sub = lax.broadcasted_iota(jnp.int32, (8, LANES), 0)
    zero8 = jnp.zeros_like(sub)
    ca = jnp.concatenate([zero8, zero8, zero8 + 1, zero8 + 2, 3 + (sub >> 2), 5 + (sub >> 1), 8 + sub], axis=0)
    cb = jnp.concatenate([sub, sub + 8, sub, sub, sub & 3, sub & 1, zero8], axis=0)
    pairs_ok = jnp.concatenate([zero8 == 0] * 5 + [sub < 6, zero8 == 0], axis=0)
    cand_ok = jnp.logical_and((ca + 1) * (cb + 1) <= PEER_TOPK, pairs_ok)
    flat_f = jnp.where(cand_ok, ca * PEER_TOPK + cb, -1).astype(F32)

    def cand_tiles(t1, t2, combine):
        lo2, hi2 = t2[0:8], t2[8:16]
        return jnp.concatenate([
            combine(t1[0:1], lo2), combine(t1[0:1], hi2), combine(t1[1:2], lo2), combine(t1[2:3], lo2),
            combine(jnp.where(sub < 4, t1[3:4], t1[4:5]), jnp.where(sub < 4, lo2, pltpu.roll(lo2, 4, 0))),
            combine(jnp.where(sub < 2, t1[5:6], jnp.where(sub < 4, t1[6:7], t1[7:8])),
                    jnp.where((sub & 1) == 0, t2[0:1], t2[1:2])),
            combine(t1[8:16], t2[0:1])], axis=0)

    def head_body(hd, carry):
        for lh in range(RT_TM // LANES):
            for c, (val_ref, idx_ref) in enumerate(((v1_scr, i1_scr), (v2_scr, i2_scr))):
                s = lax.dot_general(sk_ref[hd * 2 + c], q_scr[hd * 2 + c, pl.ds(lh * LANES, LANES), :],
                                    NT, preferred_element_type=F32)
                _top16_rows(s, key_f, val_ref, idx_ref)
            v1, i1 = v1_scr[...], i1_scr[...]
            v2, i2 = v2_scr[...], i2_scr[...]
            cand = jnp.where(cand_ok, cand_tiles(v1, v2, lambda x, y: x + y), -jnp.inf)
            cexp = cand_tiles(i1, i2, lambda x, y: x * float(PEER_NKEYS) + y)
            for r in range(PEER_TOPK):
                m = jnp.max(cand, axis=0, keepdims=True)
                sel = jnp.min(jnp.where(cand == m, flat_f, 1e9), axis=0, keepdims=True)
                hit = flat_f == sel
                cv_scr[_coef_row(r):_coef_row(r) + 1, :] = m
                ce_scr[r:r + 1, :] = jnp.max(jnp.where(hit, cexp, -1.0), axis=0, keepdims=True)
                cand = jnp.where(hit, -jnp.inf, cand)
            cv = cv_scr[...]
            w = jnp.exp(cv - cv[0:1])
            w = w / jnp.sum(w, axis=0, keepdims=True)
            rows = pl.ds(pl.multiple_of(hd * PEER_TOPK, PEER_TOPK), PEER_TOPK)
            et_scr[rows, lh * LANES:(lh + 1) * LANES] = ce_scr[...]
            g_ref[rows, lh * LANES:(lh + 1) * LANES] = w
        return carry

    lax.fori_loop(0, PEER_HEADS, head_body, 0)
    e_ref[...] = et_scr[...].T.astype(jnp.int32)


def _peer_route(x1, gffn, w_pq, sk):
    t_tok = x1.shape[0]
    small = [pltpu.VMEM((PEER_TOPK, LANES), F32) for _ in range(6)]
    return pl.pallas_call(
        _peer_route_kernel,
        grid=(t_tok // RT_TM,),
        in_specs=[pl.BlockSpec((RT_TM, D_MODEL), lambda i: (i, 0)),
                  _resident(gffn.shape), _resident(w_pq.shape), _resident(sk.shape)],
        out_specs=[pl.BlockSpec((RT_TM, PEER_SEL), lambda i: (i, 0)),
                   pl.BlockSpec((PEER_SEL, RT_TM), lambda i: (0, i))],
        out_shape=[jax.ShapeDtypeStruct((t_tok, PEER_SEL), jnp.int32),
                   jax.ShapeDtypeStruct((PEER_SEL, t_tok), F32)],
        scratch_shapes=[pltpu.VMEM((2 * PEER_HEADS, RT_TM, LANES), BF16)] + small
                       + [pltpu.VMEM((PEER_SEL, RT_TM), F32)],
        compiler_params=pltpu.CompilerParams(dimension_semantics=("arbitrary",),
                                             vmem_limit_bytes=VMEM_LIMIT),
        name="peer_route",
    )(x1, gffn, w_pq, sk)


MX_TB = 128
MX_SLOTS = 16
MX_AHEAD = 12


def _merge8(ps, sub):
    lo4, lo2, lo1 = (sub & 4) == 0, (sub & 2) == 0, (sub & 1) == 0
    q = [jnp.where(lo4, ps[2 * i], ps[2 * i + 1])
         + pltpu.roll(jnp.where(lo4, ps[2 * i + 1], ps[2 * i]), 4, 0) for i in range(4)]
    r = [jnp.where(lo2, q[2 * i] + pltpu.roll(q[2 * i], 6, 0), q[2 * i + 1] + pltpu.roll(q[2 * i + 1], 2, 0))
         for i in range(2)]
    return jnp.where(lo1, r[0] + pltpu.roll(r[0], 7, 0), r[1] + pltpu.roll(r[1], 1, 0))


def _peer_mix_kernel(e_ref, en_ref, x_ref, g_ref, gffn_ref, tab_ref, y_ref, buf, sem, h_scr, m_scr,
                     c_scr):
    step_i = pl.program_id(0)
    x = x_ref[...].reshape(MX_TB // 8, ROW_CHUNKS, 8, LANES)
    ss = jnp.sum(jnp.sum(x * x, axis=3, keepdims=True), axis=1, keepdims=True)
    h = x * lax.rsqrt(ss * (1.0 / D_MODEL) + EPS) * gffn_ref[...]
    h_scr[...] = h.reshape(MX_TB * ROW_CHUNKS, LANES)

    sub = lax.broadcasted_iota(jnp.int32, (8, LANES), 0)
    lane = lax.broadcasted_iota(jnp.int32, (PEER_SEL, MX_TB), 1)

    def row_copy(idx_ref, tok, k, slot):
        return pltpu.make_async_copy(tab_ref.at[idx_ref[tok, k]], buf.at[slot, k], sem.at[slot])

    def wait_rows(slot):
        pltpu.make_async_copy(tab_ref.at[pl.ds(0, PEER_SEL)], buf.at[slot], sem.at[slot]).wait()

    def step(s, r, idx_ref, itok, do_u, do_c, do_v):
        slot_v, slot_u, slot_i = r % MX_SLOTS, (r + 2) % MX_SLOTS, (r + MX_AHEAD) % MX_SLOTS
        par = r % 2
        if do_u:
            wait_rows(slot_u)
            hrow = h_scr[_row_slice(s + 2), :]
            hlo, hhi = hrow[0:8], hrow[8:16]
        if do_c:
            a = jnp.sum(m_scr[1 - par], axis=1, keepdims=True)
            gate = jnp.sum(jnp.where(lane == s + 1, g_ref[...], 0.0), axis=1, keepdims=True)
            coef = 0.5 * a * (1.0 + lax.erf(a * INV_SQRT2)) * gate
        zero = jnp.zeros((ROW_CHUNKS, LANES), F32)
        accs = [zero, zero, zero, zero]
        merged = []
        if idx_ref is not None:
            slot_i = slot_i + (idx_ref[itok, 0] >> 31)
        for j in range(PEER_SEL // 8):
            parts = []
            for kk in range(8):
                q = 8 * j + kk
                if idx_ref is not None:
                    row_copy(idx_ref, itok, q, slot_i).start(priority=q % 2)
                if do_u:
                    u = buf[slot_u, q, 0:ROW_CHUNKS, :].astype(F32)
                    parts.append(u[0:8] * hlo + u[8:16] * hhi)
                if do_v:
                    v = buf[slot_v, q, ROW_CHUNKS:2 * ROW_CHUNKS, :].astype(F32)
                    accs[kk % 4] = accs[kk % 4] + c_scr[par, pl.ds(_coef_row(q), 1), :] * v
            if do_u:
                merged.append(_merge8(parts, sub))
        if do_v:
            rows = _row_slice(s)
            y_ref[rows, :] = x_ref[rows, :] + ((accs[0] + accs[1]) + (accs[2] + accs[3]))
        if do_u:
            for j in range(PEER_SEL // 8):
                m_scr[par, 8 * j:8 * j + 8, :] = merged[j]
        if do_c:
            c_scr[1 - par] = jnp.broadcast_to(coef, (PEER_SEL, LANES))

    @pl.when(step_i == 0)
    def _():
        for tok in range(MX_AHEAD):
            for q in range(PEER_SEL):
                row_copy(e_ref, tok, q, tok).start(priority=q % 2)

    step(-2, MX_SLOTS - 2, None, None, True, False, False)
    step(-1, MX_SLOTS - 1, None, None, True, True, False)

    def body(it, carry):
        for r in range(MX_SLOTS):
            s = it * MX_SLOTS + r
            step(s, r, e_ref, s + MX_AHEAD, True, True, True)
        return carry

    lax.fori_loop(0, MX_TB // MX_SLOTS - 1, body, 0)
    for s in range(MX_TB - MX_SLOTS, MX_TB):
        nxt = s + MX_AHEAD - MX_TB
        src, tok = (e_ref, s + MX_AHEAD) if nxt < 0 else (en_ref, nxt)
        step(s, s % MX_SLOTS, src, tok, s + 2 < MX_TB, s + 1 < MX_TB, True)

    @pl.when(step_i == pl.num_programs(0) - 1)
    def _():
        for tok in range(MX_AHEAD):
            wait_rows(tok)


def _peer_mix(x1, e_t, g_t, gffn, table):
    t_tok = x1.shape[0]
    assert MX_SLOTS & (MX_SLOTS - 1) == 0 and MX_TB % MX_SLOTS == 0 and MX_AHEAD < MX_SLOTS
    rows = MX_TB * ROW_CHUNKS
    last_blk = t_tok // MX_SLOTS - 1
    y_rows = pl.pallas_call(
        _peer_mix_kernel,
        grid=(t_tok // MX_TB,),
        in_specs=[pl.BlockSpec((MX_TB, PEER_SEL), lambda i: (i, 0), memory_space=pltpu.SMEM),
                  pl.BlockSpec((MX_SLOTS, PEER_SEL),
                               lambda i: (jnp.minimum((i + 1) * (MX_TB // MX_SLOTS), last_blk), 0),
                               memory_space=pltpu.SMEM),
                  pl.BlockSpec((rows, LANES), lambda i: (i, 0)),
                  pl.BlockSpec((PEER_SEL, MX_TB), lambda i: (0, i)),
                  pl.BlockSpec((ROW_CHUNKS, 1, LANES), lambda i: (0, 0, 0)),
                  pl.BlockSpec(memory_space=pl.ANY)],
        out_specs=pl.BlockSpec((rows, LANES), lambda i: (i, 0)),
        out_shape=jax.ShapeDtypeStruct((t_tok * ROW_CHUNKS, LANES), F32),
        scratch_shapes=[pltpu.VMEM((MX_SLOTS, PEER_SEL, 2 * ROW_CHUNKS, LANES), BF16),
                        pltpu.SemaphoreType.DMA((MX_SLOTS,)),
                        pltpu.VMEM((rows, LANES), F32),
                        pltpu.VMEM((2, PEER_SEL, LANES), F32),
                        pltpu.VMEM((2, PEER_SEL, LANES), F32)],
        compiler_params=pltpu.CompilerParams(dimension_semantics=("arbitrary",),
                                             vmem_limit_bytes=VMEM_LIMIT),
        name="peer_mix",
    )(e_t, e_t, _to_rows(x1), g_t, gffn.reshape(ROW_CHUNKS, 1, LANES), table)
    return _from_rows(y_rows, t_tok)


def _prepare(g_mix, w_in, qn_a, kn_a, qn_b, kn_b, w_br_a, w_br_b, w_out, g_ffn, w_pq, sub_keys,
             u_emb, v_emb):
    return dict(
        gmix=g_mix.reshape(1, D_MODEL),
        w_qkv=w_in[:, :QKV_WIDTH].astype(BF16),
        w_g=w_in[:, QKV_WIDTH:].astype(BF16),
        gains=jnp.stack([qn_a, kn_a, qn_b, kn_b]),
        tables=_rope_tables(),
        w_a=w_br_a.astype(BF16), w_b=w_br_b.astype(BF16), w_o=w_out.astype(BF16),
        gffn=g_ffn.reshape(1, D_MODEL),
        w_pq=w_pq.astype(BF16),
        sk=sub_keys.reshape(2 * PEER_HEADS, PEER_NKEYS, LANES).astype(BF16),
        table=_pack_table(u_emb, v_emb),
    )


def _layer(x, p):
    n_batch, length, d = x.shape
    assert length == SEQ and d == D_MODEL
    x2 = x.reshape(n_batch * length, d)
    qkv, d16 = _in_proj(x2, p["gmix"], p["w_qkv"], p["gains"], p["tables"])
    ob = _attn_b(qkv, n_batch)
    o2, lse2 = _attn_g2(d16, n_batch)
    oa = _attn_a(qkv, o2, lse2, n_batch)
    x1 = _merge(x2, oa, ob, p["gmix"], p["w_g"], p["w_a"], p["w_b"], p["w_o"])
    e_t, g_t = _peer_route(x1, p["gffn"], p["w_pq"], p["sk"])
    y = _peer_mix(x1, e_t, g_t, p["gffn"], p["table"])
    return y.reshape(n_batch, length, d)


def kernel(x_prompt, x_sample, g_mix, w_in, qn_a, kn_a, qn_b, kn_b, w_br_a, w_br_b, w_out, g_ffn,
           w_pq, sub_keys, u_emb, v_emb):
    y_prompt, y_sample = x_prompt, x_sample
    for l in range(g_mix.shape[0]):
        p = _prepare(g_mix[l], w_in[l], qn_a[l], kn_a[l], qn_b[l], kn_b[l], w_br_a[l], w_br_b[l],
                     w_out[l], g_ffn[l], w_pq[l], sub_keys[l], u_emb[l], v_emb[l])
        y_prompt = _layer(y_prompt, p)
        y_sample = _layer(y_sample, p)
    return (y_prompt, y_sample)
```

```python
import functools

import jax
import jax.numpy as jnp
from jax import lax
from jax.experimental import pallas as pl
from jax.experimental.pallas import tpu as pltpu

F32 = jnp.float32
BF16 = jnp.bfloat16

D_MODEL = 2048
SEQ = 2048
HEAD_DIM = 128
EPS = 1e-6
GRID_W = 64
SCALE = HEAD_DIM ** -0.5
A_HEADS_PER_GROUP = 4
A_DILATIONS = (1, 4, 16)
A_HALF = 64
ROPE_THETA_A = 500000.0
ROT_A = HEAD_DIM // 4
ROPE_THETA_B = 10000.0
AXIAL_HALF = HEAD_DIM // 2
QKV_WIDTH = 6144
QA0, KA0, VA0, QB0, KB0, VB0 = 0, 12, 24, 36, 44, 46
N_QKV_HEADS = 48
PEER_HEADS = 8
PEER_NKEYS = 128
PEER_TOPK = 16
PEER_SEL = PEER_HEADS * PEER_TOPK
ROW_CHUNKS = D_MODEL // 128
NEG = -1e30
INV_SQRT2 = 0.7071067811865476

LANES = 128
SUBLANES = 8
VMEM_LIMIT = 56 * 1024 * 1024


def _to_rows(x2):
    n = x2.shape[0]
    return (x2.reshape(n // SUBLANES, SUBLANES, ROW_CHUNKS, LANES).transpose(0, 2, 1, 3)
            .reshape(n * ROW_CHUNKS, LANES))


def _from_rows(r2, n):
    return (r2.reshape(n // SUBLANES, ROW_CHUNKS, SUBLANES, LANES).transpose(0, 2, 1, 3)
            .reshape(n, ROW_CHUNKS * LANES))


def _row_slice(i):
    shift = SUBLANES.bit_length() - 1
    return pl.ds((i >> shift) * (SUBLANES * ROW_CHUNKS) + (i & (SUBLANES - 1)), ROW_CHUNKS, stride=SUBLANES)


PACK_BLOCKS = 32


def _pack_table_kernel(u_ref, v_ref, o_ref):
    for e in range(SUBLANES * PACK_BLOCKS):
        o_ref[e, 0:ROW_CHUNKS, :] = u_ref[_row_slice(e), :].astype(BF16)
        o_ref[e, ROW_CHUNKS:2 * ROW_CHUNKS, :] = v_ref[_row_slice(e), :].astype(BF16)


def _pack_table(u_emb, v_emb):
    n = u_emb.shape[0]
    experts = SUBLANES * PACK_BLOCKS
    rows = experts * ROW_CHUNKS
    return pl.pallas_call(
        _pack_table_kernel,
        grid=(n // experts,),
        in_specs=[pl.BlockSpec((rows, LANES), lambda i: (i, 0)),
                  pl.BlockSpec((rows, LANES), lambda i: (i, 0))],
        out_specs=pl.BlockSpec((experts, 2 * ROW_CHUNKS, LANES), lambda i: (i, 0, 0)),
        out_shape=jax.ShapeDtypeStruct((n, 2 * ROW_CHUNKS, LANES), BF16),
        name="pack_table",
    )(_to_rows(u_emb), _to_rows(v_emb))


def _rope_tables():
    pos = jnp.arange(SEQ, dtype=F32)
    ha = ROT_A // 2
    inv_a = ROPE_THETA_A ** (-(jnp.arange(0, ROT_A, 2, dtype=F32) / ROT_A))
    ang = pos[:, None] * inv_a[None, :]
    cos, sin = jnp.cos(ang), jnp.sin(ang)
    pad = jnp.zeros((SEQ, HEAD_DIM - ROT_A), F32)
    zh = jnp.zeros((SEQ, ha), F32)
    ca = jnp.concatenate([cos, cos, pad + 1.0], axis=1)
    s1a = jnp.concatenate([-sin, zh, pad], axis=1)
    s2a = jnp.concatenate([zh, sin, pad], axis=1)
    rows = SEQ // GRID_W
    row_ids = jnp.repeat(jnp.arange(rows), GRID_W).astype(F32)
    col_ids = jnp.tile(jnp.arange(GRID_W), rows).astype(F32)
    inv_b = ROPE_THETA_B ** (-(jnp.arange(0, AXIAL_HALF, 2, dtype=F32) / AXIAL_HALF))
    ar = row_ids[:, None] * inv_b[None, :]
    ac = col_ids[:, None] * inv_b[None, :]
    zq = jnp.zeros_like(ar)
    cb = jnp.concatenate([jnp.cos(ar), jnp.cos(ar), jnp.cos(ac), jnp.cos(ac)], axis=1)
    s1b = jnp.concatenate([-jnp.sin(ar), zq, -jnp.sin(ac), zq], axis=1)
    s2b = jnp.concatenate([zq, jnp.sin(ar), zq, jnp.sin(ac)], axis=1)
    return ca, s1a, s2a, cb, s1b, s2b


IN_TM = 256
IN_TN = 256
GAIN_QA, GAIN_KA, GAIN_QB, GAIN_KB = range(4)
HEAD_KINDS = ([(GAIN_QA, "a")] * 12 + [(GAIN_KA, "a")] * 12 + [None] * 12
              + [(GAIN_QB, "b")] * 8 + [(GAIN_KB, "b")] * 2 + [None] * 2)


DIL_G2 = A_DILATIONS[2]
G2_HEADS = tuple(base + 2 * A_HEADS_PER_GROUP + i for base in (QA0, KA0, VA0) for i in range(A_HEADS_PER_GROUP))
G2_STEPS = SEQ // DIL_G2


def _in_proj_kernel(x_ref, gmix_ref, w_ref, gain_ref, ca_ref, s1a_ref, s2a_ref,
                    cb_ref, s1b_ref, s2b_ref, o_ref, d_ref, y_scr):
    x = x_ref[...]
    ms = jnp.mean(x * x, axis=-1, keepdims=True)
    h = (x * lax.rsqrt(ms + EPS) * gmix_ref[...]).astype(BF16)
    rot = {"a": (ca_ref, s1a_ref, s2a_ref, ROT_A // 2), "b": (cb_ref, s1b_ref, s2b_ref, AXIAL_HALF // 2)}
    heads_per_dot = IN_TN // HEAD_DIM
    for j in range(QKV_WIDTH // IN_TN):
        t = jnp.dot(h, w_ref[:, j * IN_TN:(j + 1) * IN_TN], preferred_element_type=F32)
        for hh in range(heads_per_dot):
            head = j * heads_per_dot + hh
            y = t[:, hh * HEAD_DIM:(hh + 1) * HEAD_DIM]
            if HEAD_KINDS[head] is not None:
                gain, kind = HEAD_KINDS[head]
                c_ref, s1_ref, s2_ref, sh = rot[kind]
                y = y * lax.rsqrt(jnp.mean(y * y, axis=-1, keepdims=True) + EPS) * gain_ref[gain:gain + 1, :]
                y = (y * c_ref[...]
                     + pltpu.roll(y, HEAD_DIM - sh, 1) * s1_ref[...]
                     + pltpu.roll(y, sh, 1) * s2_ref[...])
            o_ref[head] = y.astype(BF16)
            if head in G2_HEADS:
                y_scr[...] = y
                for res in range(DIL_G2):
                    d_ref[G2_HEADS.index(head), 0, :, res * HEAD_DIM:(res + 1) * HEAD_DIM] = (
                        y_scr[pl.ds(res, IN_TM // DIL_G2, stride=DIL_G2), :].astype(BF16))


def _in_proj(x2, gmix, w_qkv, gains, tables):
    t_tok = x2.shape[0]
    nseq = SEQ // IN_TM
    tab_spec = pl.BlockSpec((IN_TM, HEAD_DIM), lambda i: (i % nseq, 0))
    return pl.pallas_call(
        _in_proj_kernel,
        grid=(t_tok // IN_TM,),
        in_specs=[pl.BlockSpec((IN_TM, D_MODEL), lambda i: (i, 0)),
                  _resident(gmix.shape), _resident(w_qkv.shape), _resident(gains.shape),
                  tab_spec, tab_spec, tab_spec, tab_spec, tab_spec, tab_spec],
        out_specs=[pl.BlockSpec((N_QKV_HEADS, IN_TM, HEAD_DIM), lambda i: (0, i, 0)),
                   pl.BlockSpec((len(G2_HEADS), 1, IN_TM // DIL_G2, DIL_G2 * HEAD_DIM),
                                lambda i: (0, i // nseq, i % nseq, 0))],
        out_shape=[jax.ShapeDtypeStruct((N_QKV_HEADS, t_tok, HEAD_DIM), BF16),
                   jax.ShapeDtypeStruct((len(G2_HEADS), t_tok // SEQ, G2_STEPS, DIL_G2 * HEAD_DIM), BF16)],
        scratch_shapes=[pltpu.VMEM((IN_TM, HEAD_DIM), F32)],
        compiler_params=pltpu.CompilerParams(dimension_semantics=("arbitrary",),
                                             vmem_limit_bytes=VMEM_LIMIT),
        name="in_proj",
    )(x2, gmix, w_qkv, gains, *tables)


ATT_TQ = 256
NT = (((1,), (1,)), ((), ()))


def _attn_b_kernel(q_ref, k_ref, v_ref, o_ref):
    k = k_ref[0]
    v = v_ref[0]
    for hh in range(4):
        s = lax.dot_general(q_ref[hh], k, NT, preferred_element_type=F32) * SCALE
        m = jnp.max(s, axis=-1, keepdims=True)
        p = jnp.exp(s - m)
        l = jnp.sum(p, axis=-1, keepdims=True)
        o = jnp.dot(p.astype(BF16), v, preferred_element_type=F32) / l
        o_ref[:, hh * HEAD_DIM:(hh + 1) * HEAD_DIM] = o.astype(BF16)


def _attn_b(qkv, n_batch):
    t_tok = qkv.shape[1]
    nq = SEQ // ATT_TQ
    return pl.pallas_call(
        _attn_b_kernel,
        grid=(n_batch, 2, nq),
        in_specs=[pl.BlockSpec((4, ATT_TQ, HEAD_DIM), lambda b, g, qi: (QB0 // 4 + g, b * nq + qi, 0)),
                  pl.BlockSpec((1, SEQ, HEAD_DIM), lambda b, g, qi: (KB0 + g, b, 0)),
                  pl.BlockSpec((1, SEQ, HEAD_DIM), lambda b, g, qi: (VB0 + g, b, 0))],
        out_specs=pl.BlockSpec((ATT_TQ, 4 * HEAD_DIM), lambda b, g, qi: (b * nq + qi, g)),
        out_shape=jax.ShapeDtypeStruct((t_tok, 8 * HEAD_DIM), BF16),
        compiler_params=pltpu.CompilerParams(
            dimension_semantics=("arbitrary", "arbitrary", "arbitrary"), vmem_limit_bytes=VMEM_LIMIT),
        name="attn_b",
    )(qkv, qkv, qkv)


KEY_ALIGN = 64


def _key_window(dil):
    span = A_HALF * dil
    span = -(-span // KEY_ALIGN) * KEY_ALIGN
    width = min(SEQ, ATT_TQ + 2 * span)
    assert width % LANES == 0
    return width, span


def _attn_g2_kernel(q_ref, k_ref, v_ref, o_ref, lse_ref):
    i = lax.broadcasted_iota(jnp.int32, (G2_STEPS, G2_STEPS), 0)
    j = lax.broadcasted_iota(jnp.int32, (G2_STEPS, G2_STEPS), 1)
    band = jnp.abs(i - j) <= A_HALF
    for res in range(DIL_G2):
        cols = slice(res * HEAD_DIM, (res + 1) * HEAD_DIM)
        s = lax.dot_general(q_ref[0, 0, :, cols], k_ref[0, 0, :, cols], NT, preferred_element_type=F32) * SCALE
        s = jnp.where(band, s, NEG)
        m = jnp.max(s, axis=-1, keepdims=True)
        p = jnp.exp(s - m)
        l = jnp.sum(p, axis=-1, keepdims=True)
        o = jnp.dot(p.astype(BF16), v_ref[0, 0, :, cols], preferred_element_type=F32) / l
        rows = pl.ds(res, G2_STEPS, stride=DIL_G2)
        o_ref[rows, :] = o
        lse_ref[rows, :] = jnp.broadcast_to(m + jnp.log(l), (G2_STEPS, HEAD_DIM))


def _attn_g2(d16, n_batch):
    g4 = A_HEADS_PER_GROUP
    width = DIL_G2 * HEAD_DIM

    def spec(base):
        return pl.BlockSpec((1, 1, G2_STEPS, width), lambda b, hh: (base + hh, b, 0, 0))

    out_spec = pl.BlockSpec((SEQ, HEAD_DIM), lambda b, hh: (b, hh))
    out_shape = jax.ShapeDtypeStruct((n_batch * SEQ, g4 * HEAD_DIM), F32)
    return pl.pallas_call(
        _attn_g2_kernel,
        grid=(n_batch, g4),
        in_specs=[spec(0), spec(g4), spec(2 * g4)],
        out_specs=[out_spec, out_spec],
        out_shape=[out_shape, out_shape],
        compiler_params=pltpu.CompilerParams(dimension_semantics=("arbitrary", "arbitrary"),
                                             vmem_limit_bytes=VMEM_LIMIT),
        name="attn_g2",
    )(d16, d16, d16)


def _attn_a_kernel(q0_ref, q1_ref, k0_ref, k1_ref, v0_ref, v1_ref, o2_ref, lse2_ref, o_ref):
    q0pos = pl.program_id(2) * ATT_TQ
    scores, values = [], []
    for q_ref, k_ref, v_ref, dil in ((q0_ref, k0_ref, v0_ref, A_DILATIONS[0]),
                                     (q1_ref, k1_ref, v1_ref, A_DILATIONS[1])):
        width, span = _key_window(dil)
        if width < SEQ:
            start = pl.multiple_of(jnp.clip(q0pos - span, 0, SEQ - width), KEY_ALIGN)
            kk = k_ref[0, pl.ds(start, width), :]
            vv = v_ref[0, pl.ds(start, width), :]
        else:
            start = 0
            kk = k_ref[0]
            vv = v_ref[0]
        s = lax.dot_general(q_ref[0], kk, NT, preferred_element_type=F32) * SCALE
        rel = (lax.broadcasted_iota(jnp.int32, (ATT_TQ, width), 1)
               - lax.broadcasted_iota(jnp.int32, (ATT_TQ, width), 0)) + (start - q0pos)
        valid = jnp.abs(rel) <= A_HALF * dil
        if dil > 1:
            valid = jnp.logical_and(valid, (rel & (dil - 1)) == 0)
        scores.append(jnp.where(valid, s, NEG))
        values.append(vv)
    lse2 = lse2_ref[:, 0:1]
    m = functools.reduce(jnp.maximum, [jnp.max(s, axis=-1, keepdims=True) for s in scores] + [lse2])
    w2 = jnp.exp(lse2 - m)
    l = w2
    acc = o2_ref[...] * w2
    for s, vv in zip(scores, values):
        p = jnp.exp(s - m)
        l = l + jnp.sum(p, axis=-1, keepdims=True)
        acc = acc + jnp.dot(p.astype(BF16), vv, preferred_element_type=F32)
    o_ref[...] = (acc / l).astype(BF16)


def _attn_a(qkv, o2, lse2, n_batch):
    t_tok = qkv.shape[1]
    nq = SEQ // ATT_TQ
    g4 = A_HEADS_PER_GROUP

    def q_spec(g):
        return pl.BlockSpec((1, ATT_TQ, HEAD_DIM), lambda b, hh, qi: (QA0 + g4 * g + hh, b * nq + qi, 0))

    def kv_spec(base, g):
        return pl.BlockSpec((1, SEQ, HEAD_DIM), lambda b, hh, qi: (base + g4 * g + hh, b, 0))

    tile_spec = pl.BlockSpec((ATT_TQ, HEAD_DIM), lambda b, hh, qi: (b * nq + qi, hh))
    return pl.pallas_call(
        _attn_a_kernel,
        grid=(n_batch, g4, nq),
        in_specs=[q_spec(0), q_spec(1), kv_spec(KA0, 0), kv_spec(KA0, 1), kv_spec(VA0, 0), kv_spec(VA0, 1),
                  tile_spec, tile_spec],
        out_specs=tile_spec,
        out_shape=jax.ShapeDtypeStruct((t_tok, g4 * HEAD_DIM), BF16),
        compiler_params=pltpu.CompilerParams(
            dimension_semantics=("arbitrary", "arbitrary", "arbitrary"), vmem_limit_bytes=VMEM_LIMIT),
        name="attn_a",
    )(*([qkv] * 6), o2, lse2)


MG_TM = 256
MG_CH = 512


def _sigmoid(z):
    return 1.0 / (1.0 + jnp.exp(-z))


def _merge_kernel(x_ref, oa_ref, ob_ref, gmix_ref, wg_ref, wa_ref, wb_ref, wo_ref, o_ref, m_scr):
    x = x_ref[...]
    ms = jnp.mean(x * x, axis=-1, keepdims=True)
    h = (x * lax.rsqrt(ms + EPS) * gmix_ref[...]).astype(BF16)
    oa = oa_ref[...]
    ob = ob_ref[...]
    for c in range(D_MODEL // MG_CH):
        lo, hi = c * MG_CH, (c + 1) * MG_CH
        ga = jnp.dot(h, wg_ref[:, lo:hi], preferred_element_type=F32)
        gb = jnp.dot(h, wg_ref[:, D_MODEL + lo:D_MODEL + hi], preferred_element_type=F32)
        pa = jnp.dot(oa, wa_ref[:, lo:hi], preferred_element_type=F32)
        pb = jnp.dot(ob, wb_ref[:, lo:hi], preferred_element_type=F32)
        m_scr[:, lo:hi] = (_sigmoid(ga) * pa + _sigmoid(gb) * pb).astype(BF16)
    o_ref[...] = x + jnp.dot(m_scr[...], wo_ref[...], preferred_element_type=F32)


def _resident(shape):
    nd = len(shape)
    return pl.BlockSpec(shape, lambda i: (0,) * nd, pipeline_mode=pl.Buffered(1))


def _merge(x2, oa, ob, gmix, w_g, w_a, w_b, w_o):
    t_tok = x2.shape[0]
    return pl.pallas_call(
        _merge_kernel,
        grid=(t_tok // MG_TM,),
        in_specs=[pl.BlockSpec((MG_TM, D_MODEL), lambda i: (i, 0)),
                  pl.BlockSpec((MG_TM, oa.shape[1]), lambda i: (i, 0)),
                  pl.BlockSpec((MG_TM, ob.shape[1]), lambda i: (i, 0)),
                  _resident(gmix.shape), _resident(w_g.shape), _resident(w_a.shape),
                  _resident(w_b.shape), _resident(w_o.shape)],
        out_specs=pl.BlockSpec((MG_TM, D_MODEL), lambda i: (i, 0)),
        out_shape=jax.ShapeDtypeStruct((t_tok, D_MODEL), F32),
        scratch_shapes=[pltpu.VMEM((MG_TM, D_MODEL), BF16)],
        compiler_params=pltpu.CompilerParams(dimension_semantics=("arbitrary",),
                                             vmem_limit_bytes=VMEM_LIMIT),
        name="merge",
    )(x2, oa, ob, gmix, w_g, w_a, w_b, w_o)


RT_TM = 256


def _top16_rows(s, key_f, val_ref, idx_ref):
    for r in range(PEER_TOPK):
        m = jnp.max(s, axis=0, keepdims=True)
        idx = jnp.min(jnp.where(s == m, key_f, float(PEER_NKEYS)), axis=0, keepdims=True)
        val_ref[r:r + 1, :] = m
        idx_ref[r:r + 1, :] = idx
        s = jnp.where(key_f == idx, -jnp.inf, s)


PERM8 = (0, 4, 2, 6, 1, 5, 3, 7)


def _coef_row(q):
    return (q // 8) * 8 + PERM8[q % 8]


def _peer_route_kernel(x_ref, gffn_ref, wpq_ref, sk_ref, e_ref, g_ref,
                       q_scr, v1_scr, i1_scr, v2_scr, i2_scr, cv_scr, ce_scr, et_scr):
    x = x_ref[...]
    ms = jnp.mean(x * x, axis=-1, keepdims=True)
    h = (x * lax.rsqrt(ms + EPS) * gffn_ref[...]).astype(BF16)
    q = jnp.dot(h, wpq_ref[...], preferred_element_type=F32)
    for hc in range(2 * PEER_HEADS):
        q_scr[hc] = q[:, hc * LANES:(hc + 1) * LANES].astype(BF16)

    key_f = lax.broadcasted_iota(jnp.int32, (PEER_NKEYS, LANES), 0).astype(F32)
    sub = lax.broadcasted_iota(jnp.int32, (8, LANES), 0)
    zero8 = jnp.zeros_like(sub)
    ca = jnp.concatenate([zero8, zero8, zero8 + 1, zero8 + 2, 3 + (sub >> 2), 5 + (sub >> 1), 8 + sub], axis=0)
    cb = jnp.concatenate([sub, sub + 8, sub, sub, sub & 3, sub & 1, zero8], axis=0)
    pairs_ok = jnp.concatenate([zero8 == 0] * 5 + [sub < 6, zero8 == 0], axis=0)
    cand_ok = jnp.logical_and((ca + 1) * (cb + 1) <= PEER_TOPK, pairs_ok)
    flat_f = jnp.where(cand_ok, ca * PEER_TOPK + cb, -1).astype(F32)

    def cand_tiles(t1, t2, combine):
        lo2, hi2 = t2[0:8], t2[8:16]
        return jnp.concatenate([
            combine(t1[0:1], lo2), combine(t1[0:1], hi2), combine(t1[1:2], lo2), combine(t1[2:3], lo2),
            combine(jnp.where(sub < 4, t1[3:4], t1[4:5]), jnp.where(sub < 4, lo2, pltpu.roll(lo2, 4, 0))),
            combine(jnp.where(sub < 2, t1[5:6], jnp.where(sub < 4, t1[6:7], t1[7:8])),
                    jnp.where((sub & 1) == 0, t2[0:1], t2[1:2])),
            combine(t1[8:16], t2[0:1])], axis=0)

    def head_body(hd, carry):
        for lh in range(RT_TM // LANES):
            for c, (val_ref, idx_ref) in enumerate(((v1_scr, i1_scr), (v2_scr, i2_scr))):
                s = lax.dot_general(sk_ref[hd * 2 + c], q_scr[hd * 2 + c, pl.ds(lh * LANES, LANES), :],
                                    NT, preferred_element_type=F32)
                _top16_rows(s, key_f, val_ref, idx_ref)
            v1, i1 = v1_scr[...], i1_scr[...]
            v2, i2 = v2_scr[...], i2_scr[...]
            cand = jnp.where(cand_ok, cand_tiles(v1, v2, lambda x, y: x + y), -jnp.inf)
            cexp = cand_tiles(i1, i2, lambda x, y: x * float(PEER_NKEYS) + y)
            for r in range(PEER_TOPK):
                m = jnp.max(cand, axis=0, keepdims=True)
                sel = jnp.min(jnp.where(cand == m, flat_f, 1e9), axis=0, keepdims=True)
                hit = flat_f == sel
                cv_scr[_coef_row(r):_coef_row(r) + 1, :] = m
                ce_scr[r:r + 1, :] = jnp.max(jnp.where(hit, cexp, -1.0), axis=0, keepdims=True)
                cand = jnp.where(hit, -jnp.inf, cand)
            cv = cv_scr[...]
            w = jnp.exp(cv - cv[0:1])
            w = w / jnp.sum(w, axis=0, keepdims=True)
            rows = pl.ds(pl.multiple_of(hd * PEER_TOPK, PEER_TOPK), PEER_TOPK)
            et_scr[rows, lh * LANES:(lh + 1) * LANES] = ce_scr[...]
            g_ref[rows, lh * LANES:(lh + 1) * LANES] = w
        return carry

    lax.fori_loop(0, PEER_HEADS, head_body, 0)
    e_ref[...] = et_scr[...].T.astype(jnp.int32)


def _peer_route(x1, gffn, w_pq, sk):
    t_tok = x1.shape[0]
    small = [pltpu.VMEM((PEER_TOPK, LANES), F32) for _ in range(6)]
    return pl.pallas_call(
        _peer_route_kernel,
        grid=(t_tok // RT_TM,),
        in_specs=[pl.BlockSpec((RT_TM, D_MODEL), lambda i: (i, 0)),
                  _resident(gffn.shape), _resident(w_pq.shape), _resident(sk.shape)],
        out_specs=[pl.BlockSpec((RT_TM, PEER_SEL), lambda i: (i, 0)),
                   pl.BlockSpec((PEER_SEL, RT_TM), lambda i: (0, i))],
        out_shape=[jax.ShapeDtypeStruct((t_tok, PEER_SEL), jnp.int32),
                   jax.ShapeDtypeStruct((PEER_SEL, t_tok), F32)],
        scratch_shapes=[pltpu.VMEM((2 * PEER_HEADS, RT_TM, LANES), BF16)] + small
                       + [pltpu.VMEM((PEER_SEL, RT_TM), F32)],
        compiler_params=pltpu.CompilerParams(dimension_semantics=("arbitrary",),
                                             vmem_limit_bytes=VMEM_LIMIT),
        name="peer_route",
    )(x1, gffn, w_pq, sk)


MX_TB = 128
MX_SLOTS = 16
MX_AHEAD = 12


def _merge8(ps, sub):
    lo4, lo2, lo1 = (sub & 4) == 0, (sub & 2) == 0, (sub & 1) == 0
    q = [jnp.where(lo4, ps[2 * i], ps[2 * i + 1])
         + pltpu.roll(jnp.where(lo4, ps[2 * i + 1], ps[2 * i]), 4, 0) for i in range(4)]
    r = [jnp.where(lo2, q[2 * i] + pltpu.roll(q[2 * i], 6, 0), q[2 * i + 1] + pltpu.roll(q[2 * i + 1], 2, 0))
         for i in range(2)]
    return jnp.where(lo1, r[0] + pltpu.roll(r[0], 7, 0), r[1] + pltpu.roll(r[1], 1, 0))


def _peer_mix_kernel(e_ref, en_ref, x_ref, g_ref, gffn_ref, tab_ref, y_ref, buf, sem, h_scr, m_scr,
                     c_scr):
    step_i = pl.program_id(0)
    x = x_ref[...].reshape(MX_TB // 8, ROW_CHUNKS, 8, LANES)
    ss = jnp.sum(jnp.sum(x * x, axis=3, keepdims=True), axis=1, keepdims=True)
    h = x * lax.rsqrt(ss * (1.0 / D_MODEL) + EPS) * gffn_ref[...]
    h_scr[...] = h.reshape(MX_TB * ROW_CHUNKS, LANES)

    sub = lax.broadcasted_iota(jnp.int32, (8, LANES), 0)
    lane = lax.broadcasted_iota(jnp.int32, (PEER_SEL, MX_TB), 1)

    def row_copy(idx_ref, tok, k, slot):
        return pltpu.make_async_copy(tab_ref.at[idx_ref[tok, k]], buf.at[slot, k], sem.at[slot])

    def wait_rows(slot):
        pltpu.make_async_copy(tab_ref.at[pl.ds(0, PEER_SEL)], buf.at[slot], sem.at[slot]).wait()

    def step(s, r, idx_ref, itok, do_u, do_c, do_v):
        slot_v, slot_u, slot_i = r % MX_SLOTS, (r + 2) % MX_SLOTS, (r + MX_AHEAD) % MX_SLOTS
        par = r % 2
        if do_u:
            wait_rows(slot_u)
            hrow = h_scr[_row_slice(s + 2), :]
            hlo, hhi = hrow[0:8], hrow[8:16]
        if do_c:
            a = jnp.sum(m_scr[1 - par], axis=1, keepdims=True)
            gate = jnp.sum(jnp.where(lane == s + 1, g_ref[...], 0.0), axis=1, keepdims=True)
            coef = 0.5 * a * (1.0 + lax.erf(a * INV_SQRT2)) * gate
        zero = jnp.zeros((ROW_CHUNKS, LANES), F32)
        accs = [zero, zero, zero, zero]
        merged = []
        if idx_ref is not None:
            slot_i = slot_i + (idx_ref[itok, 0] >> 31)
        for j in range(PEER_SEL // 8):
            parts = []
            for kk in range(8):
                q = 8 * j + kk
                if idx_ref is not None:
                    row_copy(idx_ref, itok, q, slot_i).start(priority=q % 2)
                if do_u:
                    u = buf[slot_u, q, 0:ROW_CHUNKS, :].astype(F32)
                    parts.append(u[0:8] * hlo + u[8:16] * hhi)
                if do_v:
                    v = buf[slot_v, q, ROW_CHUNKS:2 * ROW_CHUNKS, :].astype(F32)
                    accs[kk % 4] = accs[kk % 4] + c_scr[par, pl.ds(_coef_row(q), 1), :] * v
            if do_u:
                merged.append(_merge8(parts, sub))
        if do_v:
            rows = _row_slice(s)
            y_ref[rows, :] = x_ref[rows, :] + ((accs[0] + accs[1]) + (accs[2] + accs[3]))
        if do_u:
            for j in range(PEER_SEL // 8):
                m_scr[par, 8 * j:8 * j + 8, :] = merged[j]
        if do_c:
            c_scr[1 - par] = jnp.broadcast_to(coef, (PEER_SEL, LANES))

    @pl.when(step_i == 0)
    def _():
        for tok in range(MX_AHEAD):
            for q in range(PEER_SEL):
                row_copy(e_ref, tok, q, tok).start(priority=q % 2)

    step(-2, MX_SLOTS - 2, None, None, True, False, False)
    step(-1, MX_SLOTS - 1, None, None, True, True, False)

    def body(it, carry):
        for r in range(MX_SLOTS):
            s = it * MX_SLOTS + r
            step(s, r, e_ref, s + MX_AHEAD, True, True, True)
        return carry

    lax.fori_loop(0, MX_TB // MX_SLOTS - 1, body, 0)
    for s in range(MX_TB - MX_SLOTS, MX_TB):
        nxt = s + MX_AHEAD - MX_TB
        src, tok = (e_ref, s + MX_AHEAD) if nxt < 0 else (en_ref, nxt)
        step(s, s % MX_SLOTS, src, tok, s + 2 < MX_TB, s + 1 < MX_TB, True)

    @pl.when(step_i == pl.num_programs(0) - 1)
    def _():
        for tok in range(MX_AHEAD):
            wait_rows(tok)


def _peer_mix(x1, e_t, g_t, gffn, table):
    t_tok = x1.shape[0]
    assert MX_SLOTS & (MX_SLOTS - 1) == 0 and MX_TB % MX_SLOTS == 0 and MX_AHEAD < MX_SLOTS
    rows = MX_TB * ROW_CHUNKS
    last_blk = t_tok // MX_SLOTS - 1
    y_rows = pl.pallas_call(
        _peer_mix_kernel,
        grid=(t_tok // MX_TB,),
        in_specs=[pl.BlockSpec((MX_TB, PEER_SEL), lambda i: (i, 0), memory_space=pltpu.SMEM),
                  pl.BlockSpec((MX_SLOTS, PEER_SEL),
                               lambda i: (jnp.minimum((i + 1) * (MX_TB // MX_SLOTS), last_blk), 0),
                               memory_space=pltpu.SMEM),
                  pl.BlockSpec((rows, LANES), lambda i: (i, 0)),
                  pl.BlockSpec((PEER_SEL, MX_TB), lambda i: (0, i)),
                  pl.BlockSpec((ROW_CHUNKS, 1, LANES), lambda i: (0, 0, 0)),
                  pl.BlockSpec(memory_space=pl.ANY)],
        out_specs=pl.BlockSpec((rows, LANES), lambda i: (i, 0)),
        out_shape=jax.ShapeDtypeStruct((t_tok * ROW_CHUNKS, LANES), F32),
        scratch_shapes=[pltpu.VMEM((MX_SLOTS, PEER_SEL, 2 * ROW_CHUNKS, LANES), BF16),
                        pltpu.SemaphoreType.DMA((MX_SLOTS,)),
                        pltpu.VMEM((rows, LANES), F32),
                        pltpu.VMEM((2, PEER_SEL, LANES), F32),
                        pltpu.VMEM((2, PEER_SEL, LANES), F32)],
        compiler_params=pltpu.CompilerParams(dimension_semantics=("arbitrary",),
                                             vmem_limit_bytes=VMEM_LIMIT),
        name="peer_mix",
    )(e_t, e_t, _to_rows(x1), g_t, gffn.reshape(ROW_CHUNKS, 1, LANES), table)
    return _from_rows(y_rows, t_tok)


def _prepare(g_mix, w_in, qn_a, kn_a, qn_b, kn_b, w_br_a, w_br_b, w_out, g_ffn, w_pq, sub_keys,
             u_emb, v_emb):
    return dict(
        gmix=g_mix.reshape(1, D_MODEL),
        w_qkv=w_in[:, :QKV_WIDTH].astype(BF16),
        w_g=w_in[:, QKV_WIDTH:].astype(BF16),
        gains=jnp.stack([qn_a, kn_a, qn_b, kn_b]),
        tables=_rope_tables(),
        w_a=w_br_a.astype(BF16), w_b=w_br_b.astype(BF16), w_o=w_out.astype(BF16),
        gffn=g_ffn.reshape(1, D_MODEL),
        w_pq=w_pq.astype(BF16),
        sk=sub_keys.reshape(2 * PEER_HEADS, PEER_NKEYS, LANES).astype(BF16),
        table=_pack_table(u_emb, v_emb),
    )


def _layer(x, p):
    n_batch, length, d = x.shape
    assert length == SEQ and d == D_MODEL
    x2 = x.reshape(n_batch * length, d)
    qkv, d16 = _in_proj(x2, p["gmix"], p["w_qkv"], p["gains"], p["tables"])
    ob = _attn_b(qkv, n_batch)
    o2, lse2 = _attn_g2(d16, n_batch)
    oa = _attn_a(qkv, o2, lse2, n_batch)
    x1 = _merge(x2, oa, ob, p["gmix"], p["w_g"], p["w_a"], p["w_b"], p["w_o"])
    e_t, g_t = _peer_route(x1, p["gffn"], p["w_pq"], p["sk"])
    y = _peer_mix(x1, e_t, g_t, p["gffn"], p["table"])
    return y.reshape(n_batch, length, d)


def kernel(x_prompt, x_sample, g_mix, w_in, qn_a, kn_a, qn_b, kn_b, w_br_a, w_br_b, w_out, g_ffn,
           w_pq, sub_keys, u_emb, v_emb):
    y_prompt, y_sample = x_prompt, x_sample
    for l in range(g_mix.shape[0]):
        p = _prepare(g_mix[l], w_in[l], qn_a[l], kn_a[l], qn_b[l], kn_b[l], w_br_a[l], w_br_b[l],
                     w_out[l], g_ffn[l], w_pq[l], sub_keys[l], u_emb[l], v_emb[l])
        y_prompt = _layer(y_prompt, p)
        y_sample = _layer(y_sample, p)
    return (y_prompt, y_sample)
```

```python
import functools

import jax
import jax.numpy as jnp
from jax import lax
from jax.experimental import pallas as pl
from jax.experimental.pallas import tpu as pltpu

F32 = jnp.float32
BF16 = jnp.bfloat16

D_MODEL = 2048
SEQ = 2048
HEAD_DIM = 128
EPS = 1e-6
GRID_W = 64
SCALE = HEAD_DIM ** -0.5
A_HEADS_PER_GROUP = 4
A_DILATIONS = (1, 4, 16)
A_HALF = 64
ROPE_THETA_A = 500000.0
ROT_A = HEAD_DIM // 4
ROPE_THETA_B = 10000.0
AXIAL_HALF = HEAD_DIM // 2
QKV_WIDTH = 6144
QA0, KA0, VA0, QB0, KB0, VB0 = 0, 12, 24, 36, 44, 46
N_QKV_HEADS = 48
PEER_HEADS = 8
PEER_NKEYS = 128
PEER_TOPK = 16
PEER_SEL = PEER_HEADS * PEER_TOPK
ROW_CHUNKS = D_MODEL // 128
NEG = -1e30
INV_SQRT2 = 0.7071067811865476

LANES = 128
SUBLANES = 8
VMEM_LIMIT = 56 * 1024 * 1024


def _to_rows(x2):
    n = x2.shape[0]
    return (x2.reshape(n // SUBLANES, SUBLANES, ROW_CHUNKS, LANES).transpose(0, 2, 1, 3)
            .reshape(n * ROW_CHUNKS, LANES))


def _from_rows(r2, n):
    return (r2.reshape(n // SUBLANES, ROW_CHUNKS, SUBLANES, LANES).transpose(0, 2, 1, 3)
            .reshape(n, ROW_CHUNKS * LANES))


def _row_slice(i):
    shift = SUBLANES.bit_length() - 1
    return pl.ds((i >> shift) * (SUBLANES * ROW_CHUNKS) + (i & (SUBLANES - 1)), ROW_CHUNKS, stride=SUBLANES)


PACK_BLOCKS = 32


def _pack_table_kernel(u_ref, v_ref, o_ref):
    for e in range(SUBLANES * PACK_BLOCKS):
        o_ref[e, 0:ROW_CHUNKS, :] = u_ref[_row_slice(e), :].astype(BF16)
        o_ref[e, ROW_CHUNKS:2 * ROW_CHUNKS, :] = v_ref[_row_slice(e), :].astype(BF16)


def _pack_table(u_emb, v_emb):
    n = u_emb.shape[0]
    experts = SUBLANES * PACK_BLOCKS
    rows = experts * ROW_CHUNKS
    return pl.pallas_call(
        _pack_table_kernel,
        grid=(n // experts,),
        in_specs=[pl.BlockSpec((rows, LANES), lambda i: (i, 0)),
                  pl.BlockSpec((rows, LANES), lambda i: (i, 0))],
        out_specs=pl.BlockSpec((experts, 2 * ROW_CHUNKS, LANES), lambda i: (i, 0, 0)),
        out_shape=jax.ShapeDtypeStruct((n, 2 * ROW_CHUNKS, LANES), BF16),
        name="pack_table",
    )(_to_rows(u_emb), _to_rows(v_emb))


def _rope_tables():
    pos = jnp.arange(SEQ, dtype=F32)
    ha = ROT_A // 2
    inv_a = ROPE_THETA_A ** (-(jnp.arange(0, ROT_A, 2, dtype=F32) / ROT_A))
    ang = pos[:, None] * inv_a[None, :]
    cos, sin = jnp.cos(ang), jnp.sin(ang)
    pad = jnp.zeros((SEQ, HEAD_DIM - ROT_A), F32)
    zh = jnp.zeros((SEQ, ha), F32)
    ca = jnp.concatenate([cos, cos, pad + 1.0], axis=1)
    s1a = jnp.concatenate([-sin, zh, pad], axis=1)
    s2a = jnp.concatenate([zh, sin, pad], axis=1)
    rows = SEQ // GRID_W
    row_ids = jnp.repeat(jnp.arange(rows), GRID_W).astype(F32)
    col_ids = jnp.tile(jnp.arange(GRID_W), rows).astype(F32)
    inv_b = ROPE_THETA_B ** (-(jnp.arange(0, AXIAL_HALF, 2, dtype=F32) / AXIAL_HALF))
    ar = row_ids[:, None] * inv_b[None, :]
    ac = col_ids[:, None] * inv_b[None, :]
    zq = jnp.zeros_like(ar)
    cb = jnp.concatenate([jnp.cos(ar), jnp.cos(ar), jnp.cos(ac), jnp.cos(ac)], axis=1)
    s1b = jnp.concatenate([-jnp.sin(ar), zq, -jnp.sin(ac), zq], axis=1)
    s2b = jnp.concatenate([zq, jnp.sin(ar), zq, jnp.sin(ac)], axis=1)
    return ca, s1a, s2a, cb, s1b, s2b


IN_TM = 256
IN_TN = 256
GAIN_QA, GAIN_KA, GAIN_QB, GAIN_KB = range(4)
HEAD_KINDS = ([(GAIN_QA, "a")] * 12 + [(GAIN_KA, "a")] * 12 + [None] * 12
              + [(GAIN_QB, "b")] * 8 + [(GAIN_KB, "b")] * 2 + [None] * 2)


DIL_G2 = A_DILATIONS[2]
G2_HEADS = tuple(base + 2 * A_HEADS_PER_GROUP + i for base in (QA0, KA0, VA0) for i in range(A_HEADS_PER_GROUP))
G2_STEPS = SEQ // DIL_G2


def _in_proj_kernel(x_ref, gmix_ref, w_ref, gain_ref, ca_ref, s1a_ref, s2a_ref,
                    cb_ref, s1b_ref, s2b_ref, o_ref, d_ref, y_scr):
    x = x_ref[...]
    ms = jnp.mean(x * x, axis=-1, keepdims=True)
    h = (x * lax.rsqrt(ms + EPS) * gmix_ref[...]).astype(BF16)
    rot = {"a": (ca_ref, s1a_ref, s2a_ref, ROT_A // 2), "b": (cb_ref, s1b_ref, s2b_ref, AXIAL_HALF // 2)}
    heads_per_dot = IN_TN // HEAD_DIM
    for j in range(QKV_WIDTH // IN_TN):
        t = jnp.dot(h, w_ref[:, j * IN_TN:(j + 1) * IN_TN], preferred_element_type=F32)
        for hh in range(heads_per_dot):
            head = j * heads_per_dot + hh
            y = t[:, hh * HEAD_DIM:(hh + 1) * HEAD_DIM]
            if HEAD_KINDS[head] is not None:
                gain, kind = HEAD_KINDS[head]
                c_ref, s1_ref, s2_ref, sh = rot[kind]
                y = y * lax.rsqrt(jnp.mean(y * y, axis=-1, keepdims=True) + EPS) * gain_ref[gain:gain + 1, :]
                y = (y * c_ref[...]
                     + pltpu.roll(y, HEAD_DIM - sh, 1) * s1_ref[...]
                     + pltpu.roll(y, sh, 1) * s2_ref[...])
            o_ref[head] = y.astype(BF16)
            if head in G2_HEADS:
                y_scr[...] = y
                for res in range(DIL_G2):
                    d_ref[G2_HEADS.index(head), 0, :, res * HEAD_DIM:(res + 1) * HEAD_DIM] = (
                        y_scr[pl.ds(res, IN_TM // DIL_G2, stride=DIL_G2), :].astype(BF16))


def _in_proj(x2, gmix, w_qkv, gains, tables):
    t_tok = x2.shape[0]
    nseq = SEQ // IN_TM
    tab_spec = pl.BlockSpec((IN_TM, HEAD_DIM), lambda i: (i % nseq, 0))
    return pl.pallas_call(
        _in_proj_kernel,
        grid=(t_tok // IN_TM,),
        in_specs=[pl.BlockSpec((IN_TM, D_MODEL), lambda i: (i, 0)),
                  _resident(gmix.shape), _resident(w_qkv.shape), _resident(gains.shape),
                  tab_spec, tab_spec, tab_spec, tab_spec, tab_spec, tab_spec],
        out_specs=[pl.BlockSpec((N_QKV_HEADS, IN_TM, HEAD_DIM), lambda i: (0, i, 0)),
                   pl.BlockSpec((len(G2_HEADS), 1, IN_TM // DIL_G2, DIL_G2 * HEAD_DIM),
                                lambda i: (0, i // nseq, i % nseq, 0))],
        out_shape=[jax.ShapeDtypeStruct((N_QKV_HEADS, t_tok, HEAD_DIM), BF16),
                   jax.ShapeDtypeStruct((len(G2_HEADS), t_tok // SEQ, G2_STEPS, DIL_G2 * HEAD_DIM), BF16)],
        scratch_shapes=[pltpu.VMEM((IN_TM, HEAD_DIM), F32)],
        compiler_params=pltpu.CompilerParams(dimension_semantics=("arbitrary",),
                                             vmem_limit_bytes=VMEM_LIMIT),
        name="in_proj",
    )(x2, gmix, w_qkv, gains, *tables)


ATT_TQ = 256
NT = (((1,), (1,)), ((), ()))


def _attn_b_kernel(q_ref, k_ref, v_ref, o_ref):
    k = k_ref[0]
    v = v_ref[0]
    for hh in range(4):
        s = lax.dot_general(q_ref[hh], k, NT, preferred_element_type=F32) * SCALE
        m = jnp.max(s, axis=-1, keepdims=True)
        p = jnp.exp(s - m)
        l = jnp.sum(p, axis=-1, keepdims=True)
        o = jnp.dot(p.astype(BF16), v, preferred_element_type=F32) / l
        o_ref[:, hh * HEAD_DIM:(hh + 1) * HEAD_DIM] = o.astype(BF16)


def _attn_b(qkv, n_batch):
    t_tok = qkv.shape[1]
    nq = SEQ // ATT_TQ
    return pl.pallas_call(
        _attn_b_kernel,
        grid=(n_batch, 2, nq),
        in_specs=[pl.BlockSpec((4, ATT_TQ, HEAD_DIM), lambda b, g, qi: (QB0 // 4 + g, b * nq + qi, 0)),
                  pl.BlockSpec((1, SEQ, HEAD_DIM), lambda b, g, qi: (KB0 + g, b, 0)),
                  pl.BlockSpec((1, SEQ, HEAD_DIM), lambda b, g, qi: (VB0 + g, b, 0))],
        out_specs=pl.BlockSpec((ATT_TQ, 4 * HEAD_DIM), lambda b, g, qi: (b * nq + qi, g)),
        out_shape=jax.ShapeDtypeStruct((t_tok, 8 * HEAD_DIM), BF16),
        compiler_params=pltpu.CompilerParams(
            dimension_semantics=("arbitrary", "arbitrary", "arbitrary"), vmem_limit_bytes=VMEM_LIMIT),
        name="attn_b",
    )(qkv, qkv, qkv)


KEY_ALIGN = 64


def _key_window(dil):
    span = A_HALF * dil
    span = -(-span // KEY_ALIGN) * KEY_ALIGN
    width = min(SEQ, ATT_TQ + 2 * span)
    assert width % LANES == 0
    return width, span


def _attn_g2_kernel(q_ref, k_ref, v_ref, o_ref, lse_ref):
    i = lax.broadcasted_iota(jnp.int32, (G2_STEPS, G2_STEPS), 0)
    j = lax.broadcasted_iota(jnp.int32, (G2_STEPS, G2_STEPS), 1)
    band = jnp.abs(i - j) <= A_HALF
    for res in range(DIL_G2):
        cols = slice(res * HEAD_DIM, (res + 1) * HEAD_DIM)
        s = lax.dot_general(q_ref[0, 0, :, cols], k_ref[0, 0, :, cols], NT, preferred_element_type=F32) * SCALE
        s = jnp.where(band, s, NEG)
        m = jnp.max(s, axis=-1, keepdims=True)
        p = jnp.exp(s - m)
        l = jnp.sum(p, axis=-1, keepdims=True)
        o = jnp.dot(p.astype(BF16), v_ref[0, 0, :, cols], preferred_element_type=F32) / l
        rows = pl.ds(res, G2_STEPS, stride=DIL_G2)
        o_ref[rows, :] = o
        lse_ref[rows, :] = jnp.broadcast_to(m + jnp.log(l), (G2_STEPS, HEAD_DIM))


def _attn_g2(d16, n_batch):
    g4 = A_HEADS_PER_GROUP
    width = DIL_G2 * HEAD_DIM

    def spec(base):
        return pl.BlockSpec((1, 1, G2_STEPS, width), lambda b, hh: (base + hh, b, 0, 0))

    out_spec = pl.BlockSpec((SEQ, HEAD_DIM), lambda b, hh: (b, hh))
    out_shape = jax.ShapeDtypeStruct((n_batch * SEQ, g4 * HEAD_DIM), F32)
    return pl.pallas_call(
        _attn_g2_kernel,
        grid=(n_batch, g4),
        in_specs=[spec(0), spec(g4), spec(2 * g4)],
        out_specs=[out_spec, out_spec],
        out_shape=[out_shape, out_shape],
        compiler_params=pltpu.CompilerParams(dimension_semantics=("arbitrary", "arbitrary"),
                                             vmem_limit_bytes=VMEM_LIMIT),
        name="attn_g2",
    )(d16, d16, d16)


def _attn_a_kernel(q0_ref, q1_ref, k0_ref, k1_ref, v0_ref, v1_ref, o2_ref, lse2_ref, o_ref):
    q0pos = pl.program_id(1) * ATT_TQ
    for hh in range(A_HEADS_PER_GROUP):
        cols = slice(hh * HEAD_DIM, (hh + 1) * HEAD_DIM)
        scores, values = [], []
        for q_ref, k_ref, v_ref, dil in ((q0_ref, k0_ref, v0_ref, A_DILATIONS[0]),
                                         (q1_ref, k1_ref, v1_ref, A_DILATIONS[1])):
            width, span = _key_window(dil)
            start = pl.multiple_of(jnp.clip(q0pos - span, 0, SEQ - width), KEY_ALIGN)
            kk = k_ref[hh, pl.ds(start, width), :]
            vv = v_ref[hh, pl.ds(start, width), :]
            s = lax.dot_general(q_ref[hh], kk, NT, preferred_element_type=F32) * SCALE
            rel = (lax.broadcasted_iota(jnp.int32, (ATT_TQ, width), 1)
                   - lax.broadcasted_iota(jnp.int32, (ATT_TQ, width), 0)) + (start - q0pos)
            valid = jnp.abs(rel) <= A_HALF * dil
            if dil > 1:
                valid = jnp.logical_and(valid, (rel & (dil - 1)) == 0)
            scores.append(jnp.where(valid, s, NEG))
            values.append(vv)
        lse2 = lse2_ref[:, hh * HEAD_DIM:hh * HEAD_DIM + 1]
        m = functools.reduce(jnp.maximum, [jnp.max(s, axis=-1, keepdims=True) for s in scores] + [lse2])
        w2 = jnp.exp(lse2 - m)
        l = w2
        acc = o2_ref[:, cols] * w2
        for s, vv in zip(scores, values):
            p = jnp.exp(s - m)
            l = l + jnp.sum(p, axis=-1, keepdims=True)
            acc = acc + jnp.dot(p.astype(BF16), vv, preferred_element_type=F32)
        o_ref[:, cols] = (acc / l).astype(BF16)


def _attn_a(qkv, o2, lse2, n_batch):
    t_tok = qkv.shape[1]
    nq = SEQ // ATT_TQ
    g4 = A_HEADS_PER_GROUP
    assert all(_key_window(d)[0] < SEQ for d in A_DILATIONS[:2])

    def q_spec(g):
        return pl.BlockSpec((g4, ATT_TQ, HEAD_DIM), lambda b, qi: (QA0 // g4 + g, b * nq + qi, 0))

    def kv_spec(base, g):
        return pl.BlockSpec((g4, SEQ, HEAD_DIM), lambda b, qi: (base // g4 + g, b, 0))

    tile_spec = pl.BlockSpec((ATT_TQ, g4 * HEAD_DIM), lambda b, qi: (b * nq + qi, 0))
    return pl.pallas_call(
        _attn_a_kernel,
        grid=(n_batch, nq),
        in_specs=[q_spec(0), q_spec(1), kv_spec(KA0, 0), kv_spec(KA0, 1), kv_spec(VA0, 0), kv_spec(VA0, 1),
                  tile_spec, tile_spec],
        out_specs=tile_spec,
        out_shape=jax.ShapeDtypeStruct((t_tok, g4 * HEAD_DIM), BF16),
        compiler_params=pltpu.CompilerParams(
            dimension_semantics=("arbitrary", "arbitrary"), vmem_limit_bytes=VMEM_LIMIT),
        name="attn_a",
    )(*([qkv] * 6), o2, lse2)


MG_TM = 256
MG_CH = 512


def _sigmoid(z):
    return 1.0 / (1.0 + jnp.exp(-z))


def _merge_kernel(x_ref, oa_ref, ob_ref, gmix_ref, wg_ref, wa_ref, wb_ref, wo_ref, o_ref, m_scr):
    x = x_ref[...]
    ms = jnp.mean(x * x, axis=-1, keepdims=True)
    h = (x * lax.rsqrt(ms + EPS) * gmix_ref[...]).astype(BF16)
    oa = oa_ref[...]
    ob = ob_ref[...]
    for c in range(D_MODEL // MG_CH):
        lo, hi = c * MG_CH, (c + 1) * MG_CH
        ga = jnp.dot(h, wg_ref[:, lo:hi], preferred_element_type=F32)
        gb = jnp.dot(h, wg_ref[:, D_MODEL + lo:D_MODEL + hi], preferred_element_type=F32)
        pa = jnp.dot(oa, wa_ref[:, lo:hi], preferred_element_type=F32)
        pb = jnp.dot(ob, wb_ref[:, lo:hi], preferred_element_type=F32)
        m_scr[:, lo:hi] = (_sigmoid(ga) * pa + _sigmoid(gb) * pb).astype(BF16)
    o_ref[...] = x + jnp.dot(m_scr[...], wo_ref[...], preferred_element_type=F32)


def _resident(shape):
    nd = len(shape)
    return pl.BlockSpec(shape, lambda i: (0,) * nd, pipeline_mode=pl.Buffered(1))


def _merge(x2, oa, ob, gmix, w_g, w_a, w_b, w_o):
    t_tok = x2.shape[0]
    return pl.pallas_call(
        _merge_kernel,
        grid=(t_tok // MG_TM,),
        in_specs=[pl.BlockSpec((MG_TM, D_MODEL), lambda i: (i, 0)),
                  pl.BlockSpec((MG_TM, oa.shape[1]), lambda i: (i, 0)),
                  pl.BlockSpec((MG_TM, ob.shape[1]), lambda i: (i, 0)),
                  _resident(gmix.shape), _resident(w_g.shape), _resident(w_a.shape),
                  _resident(w_b.shape), _resident(w_o.shape)],
        out_specs=pl.BlockSpec((MG_TM, D_MODEL), lambda i: (i, 0)),
        out_shape=jax.ShapeDtypeStruct((t_tok, D_MODEL), F32),
        scratch_shapes=[pltpu.VMEM((MG_TM, D_MODEL), BF16)],
        compiler_params=pltpu.CompilerParams(dimension_semantics=("arbitrary",),
                                             vmem_limit_bytes=VMEM_LIMIT),
        name="merge",
    )(x2, oa, ob, gmix, w_g, w_a, w_b, w_o)


RT_TM = 256


def _top16_rows(s, key_f, val_ref, idx_ref):
    for r in range(PEER_TOPK):
        m = jnp.max(s, axis=0, keepdims=True)
        idx = jnp.min(jnp.where(s == m, key_f, float(PEER_NKEYS)), axis=0, keepdims=True)
        val_ref[r:r + 1, :] = m
        idx_ref[r:r + 1, :] = idx
        s = jnp.where(key_f == idx, -jnp.inf, s)


PERM8 = (0, 4, 2, 6, 1, 5, 3, 7)


def _coef_row(q):
    return (q // 8) * 8 + PERM8[q % 8]


def _peer_route_kernel(x_ref, gffn_ref, wpq_ref, sk_ref, e_ref, g_ref,
                       q_scr, v1_scr, i1_scr, v2_scr, i2_scr, cv_scr, ce_scr, et_scr):
    x = x_ref[...]
    ms = jnp.mean(x * x, axis=-1, keepdims=True)
    h = (x * lax.rsqrt(ms + EPS) * gffn_ref[...]).astype(BF16)
    q = jnp.dot(h, wpq_ref[...], preferred_element_type=F32)
    for hc in range(2 * PEER_HEADS):
        q_scr[hc] = q[:, hc * LANES:(hc + 1) * LANES].astype(BF16)

    key_f = lax.broadcasted_iota(jnp.int32, (PEER_NKEYS, LANES), 0).astype(F32)
    sub = lax.broadcasted_iota(jnp.int32, (8, LANES), 0)
    zero8 = jnp.zeros_like(sub)
    ca = jnp.concatenate([zero8, zero8, zero8 + 1, zero8 + 2, 3 + (sub >> 2), 5 + (sub >> 1), 8 + sub], axis=0)
    cb = jnp.concatenate([sub, sub + 8, sub, sub, sub & 3, sub & 1, zero8], axis=0)
    pairs_ok = jnp.concatenate([zero8 == 0] * 5 + [sub < 6, zero8 == 0], axis=0)
    cand_ok = jnp.logical_and((ca + 1) * (cb + 1) <= PEER_TOPK, pairs_ok)
    flat_f = jnp.where(cand_ok, ca * PEER_TOPK + cb, -1).astype(F32)

    def cand_tiles(t1, t2, combine):
        lo2, hi2 = t2[0:8], t2[8:16]
        return jnp.concatenate([
            combine(t1[0:1], lo2), combine(t1[0:1], hi2), combine(t1[1:2], lo2), combine(t1[2:3], lo2),
            combine(jnp.where(sub < 4, t1[3:4], t1[4:5]), jnp.where(sub < 4, lo2, pltpu.roll(lo2, 4, 0))),
            combine(jnp.where(sub < 2, t1[5:6], jnp.where(sub < 4, t1[6:7], t1[7:8])),
                    jnp.where((sub & 1) == 0, t2[0:1], t2[1:2])),
            combine(t1[8:16], t2[0:1])], axis=0)

    def head_body(hd, carry):
        for lh in range(RT_TM // LANES):
            for c, (val_ref, idx_ref) in enumerate(((v1_scr, i1_scr), (v2_scr, i2_scr))):
                s = lax.dot_general(sk_ref[hd * 2 + c], q_scr[hd * 2 + c, pl.ds(lh * LANES, LANES), :],
                                    NT, preferred_element_type=F32)
                _top16_rows(s, key_f, val_ref, idx_ref)
            v1, i1 = v1_scr[...], i1_scr[...]
            v2, i2 = v2_scr[...], i2_scr[...]
            cand = jnp.where(cand_ok, cand_tiles(v1, v2, lambda x, y: x + y), -jnp.inf)
            cexp = cand_tiles(i1, i2, lambda x, y: x * float(PEER_NKEYS) + y)
            for r in range(PEER_TOPK):
                m = jnp.max(cand, axis=0, keepdims=True)
                sel = jnp.min(jnp.where(cand == m, flat_f, 1e9), axis=0, keepdims=True)
                hit = flat_f == sel
                cv_scr[_coef_row(r):_coef_row(r) + 1, :] = m
                ce_scr[r:r + 1, :] = jnp.max(jnp.where(hit, cexp, -1.0), axis=0, keepdims=True)
                cand = jnp.where(hit, -jnp.inf, cand)
            cv = cv_scr[...]
            w = jnp.exp(cv - cv[0:1])
            w = w / jnp.sum(w, axis=0, keepdims=True)
            rows = pl.ds(pl.multiple_of(hd * PEER_TOPK, PEER_TOPK), PEER_TOPK)
            et_scr[rows, lh * LANES:(lh + 1) * LANES] = ce_scr[...]
            g_ref[rows, lh * LANES:(lh + 1) * LANES] = w
        return carry

    lax.fori_loop(0, PEER_HEADS, head_body, 0)
    e_ref[...] = et_scr[...].T.astype(jnp.int32)


def _peer_route(x1, gffn, w_pq, sk):
    t_tok = x1.shape[0]
    small = [pltpu.VMEM((PEER_TOPK, LANES), F32) for _ in range(6)]
    return pl.pallas_call(
        _peer_route_kernel,
        grid=(t_tok // RT_TM,),
        in_specs=[pl.BlockSpec((RT_TM, D_MODEL), lambda i: (i, 0)),
                  _resident(gffn.shape), _resident(w_pq.shape), _resident(sk.shape)],
        out_specs=[pl.BlockSpec((RT_TM, PEER_SEL), lambda i: (i, 0)),
                   pl.BlockSpec((PEER_SEL, RT_TM), lambda i: (0, i))],
        out_shape=[jax.ShapeDtypeStruct((t_tok, PEER_SEL), jnp.int32),
                   jax.ShapeDtypeStruct((PEER_SEL, t_tok), F32)],
        scratch_shapes=[pltpu.VMEM((2 * PEER_HEADS, RT_TM, LANES), BF16)] + small
                       + [pltpu.VMEM((PEER_SEL, RT_TM), F32)],
        compiler_params=pltpu.CompilerParams(dimension_semantics=("arbitrary",),
                                             vmem_limit_bytes=VMEM_LIMIT),
        name="peer_route",
    )(x1, gffn, w_pq, sk)


MX_TB = 128
MX_SLOTS = 16
MX_AHEAD = 12


def _merge8(ps, sub):
    lo4, lo2, lo1 = (sub & 4) == 0, (sub & 2) == 0, (sub & 1) == 0
    q = [jnp.where(lo4, ps[2 * i], ps[2 * i + 1])
         + pltpu.roll(jnp.where(lo4, ps[2 * i + 1], ps[2 * i]), 4, 0) for i in range(4)]
    r = [jnp.where(lo2, q[2 * i] + pltpu.roll(q[2 * i], 6, 0), q[2 * i + 1] + pltpu.roll(q[2 * i + 1], 2, 0))
         for i in range(2)]
    return jnp.where(lo1, r[0] + pltpu.roll(r[0], 7, 0), r[1] + pltpu.roll(r[1], 1, 0))


def _peer_mix_kernel(e_ref, en_ref, x_ref, g_ref, gffn_ref, tab_ref, y_ref, buf, sem, h_scr, m_scr,
                     c_scr):
    step_i = pl.program_id(0)
    x = x_ref[...].reshape(MX_TB // 8, ROW_CHUNKS, 8, LANES)
    ss = jnp.sum(jnp.sum(x * x, axis=3, keepdims=True), axis=1, keepdims=True)
    h = x * lax.rsqrt(ss * (1.0 / D_MODEL) + EPS) * gffn_ref[...]
    h_scr[...] = h.reshape(MX_TB * ROW_CHUNKS, LANES)

    sub = lax.broadcasted_iota(jnp.int32, (8, LANES), 0)
    lane = lax.broadcasted_iota(jnp.int32, (PEER_SEL, MX_TB), 1)

    def row_copy(idx_ref, tok, k, slot):
        return pltpu.make_async_copy(tab_ref.at[idx_ref[tok, k]], buf.at[slot, k], sem.at[slot])

    def wait_rows(slot):
        pltpu.make_async_copy(tab_ref.at[pl.ds(0, PEER_SEL)], buf.at[slot], sem.at[slot]).wait()

    def step(s, r, idx_ref, itok, do_u, do_c, do_v):
        slot_v, slot_u, slot_i = r % MX_SLOTS, (r + 2) % MX_SLOTS, (r + MX_AHEAD) % MX_SLOTS
        par = r % 2
        if do_u:
            wait_rows(slot_u)
            hrow = h_scr[_row_slice(s + 2), :]
            hlo, hhi = hrow[0:8], hrow[8:16]
        if do_c:
            a = jnp.sum(m_scr[1 - par], axis=1, keepdims=True)
            gate = jnp.sum(jnp.where(lane == s + 1, g_ref[...], 0.0), axis=1, keepdims=True)
            coef = 0.5 * a * (1.0 + lax.erf(a * INV_SQRT2)) * gate
        zero = jnp.zeros((ROW_CHUNKS, LANES), F32)
        accs = [zero, zero, zero, zero]
        merged = []
        if idx_ref is not None:
            slot_i = slot_i + (idx_ref[itok, 0] >> 31)
        for j in range(PEER_SEL // 8):
            parts = []
            for kk in range(8):
                q = 8 * j + kk
                if idx_ref is not None:
                    row_copy(idx_ref, itok, q, slot_i).start(priority=q % 2)
                if do_u:
                    u = buf[slot_u, q, 0:ROW_CHUNKS, :].astype(F32)
                    parts.append(u[0:8] * hlo + u[8:16] * hhi)
                if do_v:
                    v = buf[slot_v, q, ROW_CHUNKS:2 * ROW_CHUNKS, :].astype(F32)
                    accs[kk % 4] = accs[kk % 4] + c_scr[par, pl.ds(_coef_row(q), 1), :] * v
            if do_u:
                merged.append(_merge8(parts, sub))
        if do_v:
            rows = _row_slice(s)
            y_ref[rows, :] = x_ref[rows, :] + ((accs[0] + accs[1]) + (accs[2] + accs[3]))
        if do_u:
            for j in range(PEER_SEL // 8):
                m_scr[par, 8 * j:8 * j + 8, :] = merged[j]
        if do_c:
            c_scr[1 - par] = jnp.broadcast_to(coef, (PEER_SEL, LANES))

    @pl.when(step_i == 0)
    def _():
        for tok in range(MX_AHEAD):
            for q in range(PEER_SEL):
                row_copy(e_ref, tok, q, tok).start(priority=q % 2)

    step(-2, MX_SLOTS - 2, None, None, True, False, False)
    step(-1, MX_SLOTS - 1, None, None, True, True, False)

    def body(it, carry):
        for r in range(MX_SLOTS):
            s = it * MX_SLOTS + r
            step(s, r, e_ref, s + MX_AHEAD, True, True, True)
        return carry

    lax.fori_loop(0, MX_TB // MX_SLOTS - 1, body, 0)
    for s in range(MX_TB - MX_SLOTS, MX_TB):
        nxt = s + MX_AHEAD - MX_TB
        src, tok = (e_ref, s + MX_AHEAD) if nxt < 0 else (en_ref, nxt)
        step(s, s % MX_SLOTS, src, tok, s + 2 < MX_TB, s + 1 < MX_TB, True)

    @pl.when(step_i == pl.num_programs(0) - 1)
    def _():
        for tok in range(MX_AHEAD):
            wait_rows(tok)


def _peer_mix(x1, e_t, g_t, gffn, table):
    t_tok = x1.shape[0]
    assert MX_SLOTS & (MX_SLOTS - 1) == 0 and MX_TB % MX_SLOTS == 0 and MX_AHEAD < MX_SLOTS
    rows = MX_TB * ROW_CHUNKS
    last_blk = t_tok // MX_SLOTS - 1
    y_rows = pl.pallas_call(
        _peer_mix_kernel,
        grid=(t_tok // MX_TB,),
        in_specs=[pl.BlockSpec((MX_TB, PEER_SEL), lambda i: (i, 0), memory_space=pltpu.SMEM),
                  pl.BlockSpec((MX_SLOTS, PEER_SEL),
                               lambda i: (jnp.minimum((i + 1) * (MX_TB // MX_SLOTS), last_blk), 0),
                               memory_space=pltpu.SMEM),
                  pl.BlockSpec((rows, LANES), lambda i: (i, 0)),
                  pl.BlockSpec((PEER_SEL, MX_TB), lambda i: (0, i)),
                  pl.BlockSpec((ROW_CHUNKS, 1, LANES), lambda i: (0, 0, 0)),
                  pl.BlockSpec(memory_space=pl.ANY)],
        out_specs=pl.BlockSpec((rows, LANES), lambda i: (i, 0)),
        out_shape=jax.ShapeDtypeStruct((t_tok * ROW_CHUNKS, LANES), F32),
        scratch_shapes=[pltpu.VMEM((MX_SLOTS, PEER_SEL, 2 * ROW_CHUNKS, LANES), BF16),
                        pltpu.SemaphoreType.DMA((MX_SLOTS,)),
                        pltpu.VMEM((rows, LANES), F32),
                        pltpu.VMEM((2, PEER_SEL, LANES), F32),
                        pltpu.VMEM((2, PEER_SEL, LANES), F32)],
        compiler_params=pltpu.CompilerParams(dimension_semantics=("arbitrary",),
                                             vmem_limit_bytes=VMEM_LIMIT),
        name="peer_mix",
    )(e_t, e_t, _to_rows(x1), g_t, gffn.reshape(ROW_CHUNKS, 1, LANES), table)
    return _from_rows(y_rows, t_tok)


def _prepare(g_mix, w_in, qn_a, kn_a, qn_b, kn_b, w_br_a, w_br_b, w_out, g_ffn, w_pq, sub_keys,
             u_emb, v_emb):
    return dict(
        gmix=g_mix.reshape(1, D_MODEL),
        w_qkv=w_in[:, :QKV_WIDTH].astype(BF16),
        w_g=w_in[:, QKV_WIDTH:].astype(BF16),
        gains=jnp.stack([qn_a, kn_a, qn_b, kn_b]),
        tables=_rope_tables(),
        w_a=w_br_a.astype(BF16), w_b=w_br_b.astype(BF16), w_o=w_out.astype(BF16),
        gffn=g_ffn.reshape(1, D_MODEL),
        w_pq=w_pq.astype(BF16),
        sk=sub_keys.reshape(2 * PEER_HEADS, PEER_NKEYS, LANES).astype(BF16),
        table=_pack_table(u_emb, v_emb),
    )


def _layer(x, p):
    n_batch, length, d = x.shape
    assert length == SEQ and d == D_MODEL
    x2 = x.reshape(n_batch * length, d)
    qkv, d16 = _in_proj(x2, p["gmix"], p["w_qkv"], p["gains"], p["tables"])
    ob = _attn_b(qkv, n_batch)
    o2, lse2 = _attn_g2(d16, n_batch)
    oa = _attn_a(qkv, o2, lse2, n_batch)
    x1 = _merge(x2, oa, ob, p["gmix"], p["w_g"], p["w_a"], p["w_b"], p["w_o"])
    e_t, g_t = _peer_route(x1, p["gffn"], p["w_pq"], p["sk"])
    y = _peer_mix(x1, e_t, g_t, p["gffn"], p["table"])
    return y.reshape(n_batch, length, d)


def kernel(x_prompt, x_sample, g_mix, w_in, qn_a, kn_a, qn_b, kn_b, w_br_a, w_br_b, w_out, g_ffn,
           w_pq, sub_keys, u_emb, v_emb):
    y_prompt, y_sample = x_prompt, x_sample
    for l in range(g_mix.shape[0]):
        p = _prepare(g_mix[l], w_in[l], qn_a[l], kn_a[l], qn_b[l], kn_b[l], w_br_a[l], w_br_b[l],
                     w_out[l], g_ffn[l], w_pq[l], sub_keys[l], u_emb[l], v_emb[l])
        y_prompt = _layer(y_prompt, p)
        y_sample = _layer(y_sample, p)
    return (y_prompt, y_sample)
```

```python
import functools

import jax
import jax.numpy as jnp
from jax import lax
from jax.experimental import pallas as pl
from jax.experimental.pallas import tpu as pltpu

F32 = jnp.float32
BF16 = jnp.bfloat16

D_MODEL = 2048
SEQ = 2048
HEAD_DIM = 128
EPS = 1e-6
GRID_W = 64
SCALE = HEAD_DIM ** -0.5
A_HEADS_PER_GROUP = 4
A_DILATIONS = (1, 4, 16)
A_HALF = 64
ROPE_THETA_A = 500000.0
ROT_A = HEAD_DIM // 4
ROPE_THETA_B = 10000.0
AXIAL_HALF = HEAD_DIM // 2
QKV_WIDTH = 6144
QA0, KA0, VA0, QB0, KB0, VB0 = 0, 12, 24, 36, 44, 46
N_QKV_HEADS = 48
PEER_HEADS = 8
PEER_NKEYS = 128
PEER_TOPK = 16
PEER_SEL = PEER_HEADS * PEER_TOPK
ROW_CHUNKS = D_MODEL // 128
NEG = -1e30
INV_SQRT2 = 0.7071067811865476

LANES = 128
SUBLANES = 8
VMEM_LIMIT = 56 * 1024 * 1024


def _to_rows(x2):
    n = x2.shape[0]
    return (x2.reshape(n // SUBLANES, SUBLANES, ROW_CHUNKS, LANES).transpose(0, 2, 1, 3)
            .reshape(n * ROW_CHUNKS, LANES))


def _from_rows(r2, n):
    return (r2.reshape(n // SUBLANES, ROW_CHUNKS, SUBLANES, LANES).transpose(0, 2, 1, 3)
            .reshape(n, ROW_CHUNKS * LANES))


def _row_slice(i):
    shift = SUBLANES.bit_length() - 1
    return pl.ds((i >> shift) * (SUBLANES * ROW_CHUNKS) + (i & (SUBLANES - 1)), ROW_CHUNKS, stride=SUBLANES)


PACK_BLOCKS = 32


def _pack_table_kernel(u_ref, v_ref, o_ref):
    for e in range(SUBLANES * PACK_BLOCKS):
        o_ref[e, 0:ROW_CHUNKS, :] = u_ref[_row_slice(e), :].astype(BF16)
        o_ref[e, ROW_CHUNKS:2 * ROW_CHUNKS, :] = v_ref[_row_slice(e), :].astype(BF16)


def _pack_table(u_emb, v_emb):
    n = u_emb.shape[0]
    experts = SUBLANES * PACK_BLOCKS
    rows = experts * ROW_CHUNKS
    return pl.pallas_call(
        _pack_table_kernel,
        grid=(n // experts,),
        in_specs=[pl.BlockSpec((rows, LANES), lambda i: (i, 0)),
                  pl.BlockSpec((rows, LANES), lambda i: (i, 0))],
        out_specs=pl.BlockSpec((experts, 2 * ROW_CHUNKS, LANES), lambda i: (i, 0, 0)),
        out_shape=jax.ShapeDtypeStruct((n, 2 * ROW_CHUNKS, LANES), BF16),
        name="pack_table",
    )(_to_rows(u_emb), _to_rows(v_emb))


def _rope_tables():
    pos = jnp.arange(SEQ, dtype=F32)
    ha = ROT_A // 2
    inv_a = ROPE_THETA_A ** (-(jnp.arange(0, ROT_A, 2, dtype=F32) / ROT_A))
    ang = pos[:, None] * inv_a[None, :]
    cos, sin = jnp.cos(ang), jnp.sin(ang)
    pad = jnp.zeros((SEQ, HEAD_DIM - ROT_A), F32)
    zh = jnp.zeros((SEQ, ha), F32)
    ca = jnp.concatenate([cos, cos, pad + 1.0], axis=1)
    s1a = jnp.concatenate([-sin, zh, pad], axis=1)
    s2a = jnp.concatenate([zh, sin, pad], axis=1)
    rows = SEQ // GRID_W
    row_ids = jnp.repeat(jnp.arange(rows), GRID_W).astype(F32)
    col_ids = jnp.tile(jnp.arange(GRID_W), rows).astype(F32)
    inv_b = ROPE_THETA_B ** (-(jnp.arange(0, AXIAL_HALF, 2, dtype=F32) / AXIAL_HALF))
    ar = row_ids[:, None] * inv_b[None, :]
    ac = col_ids[:, None] * inv_b[None, :]
    zq = jnp.zeros_like(ar)
    cb = jnp.concatenate([jnp.cos(ar), jnp.cos(ar), jnp.cos(ac), jnp.cos(ac)], axis=1)
    s1b = jnp.concatenate([-jnp.sin(ar), zq, -jnp.sin(ac), zq], axis=1)
    s2b = jnp.concatenate([zq, jnp.sin(ar), zq, jnp.sin(ac)], axis=1)
    return ca, s1a, s2a, cb, s1b, s2b


IN_TM = 256
IN_TN = 256
GAIN_QA, GAIN_KA, GAIN_QB, GAIN_KB = range(4)
HEAD_KINDS = ([(GAIN_QA, "a")] * 12 + [(GAIN_KA, "a")] * 12 + [None] * 12
              + [(GAIN_QB, "b")] * 8 + [(GAIN_KB, "b")] * 2 + [None] * 2)


DIL_G2 = A_DILATIONS[2]
G2_HEADS = tuple(base + 2 * A_HEADS_PER_GROUP + i for base in (QA0, KA0, VA0) for i in range(A_HEADS_PER_GROUP))
G2_STEPS = SEQ // DIL_G2


def _in_proj_kernel(x_ref, gmix_ref, w_ref, gain_ref, ca_ref, s1a_ref, s2a_ref,
                    cb_ref, s1b_ref, s2b_ref, o_ref, d_ref, y_scr):
    x = x_ref[...]
    ms = jnp.mean(x * x, axis=-1, keepdims=True)
    h = (x * lax.rsqrt(ms + EPS) * gmix_ref[...]).astype(BF16)
    rot = {"a": (ca_ref, s1a_ref, s2a_ref, ROT_A // 2), "b": (cb_ref, s1b_ref, s2b_ref, AXIAL_HALF // 2)}
    heads_per_dot = IN_TN // HEAD_DIM
    for j in range(QKV_WIDTH // IN_TN):
        t = jnp.dot(h, w_ref[:, j * IN_TN:(j + 1) * IN_TN], preferred_element_type=F32)
        for hh in range(heads_per_dot):
            head = j * heads_per_dot + hh
            y = t[:, hh * HEAD_DIM:(hh + 1) * HEAD_DIM]
            if HEAD_KINDS[head] is not None:
                gain, kind = HEAD_KINDS[head]
                c_ref, s1_ref, s2_ref, sh = rot[kind]
                y = y * lax.rsqrt(jnp.mean(y * y, axis=-1, keepdims=True) + EPS) * gain_ref[gain:gain + 1, :]
                y = (y * c_ref[...]
                     + pltpu.roll(y, HEAD_DIM - sh, 1) * s1_ref[...]
                     + pltpu.roll(y, sh, 1) * s2_ref[...])
            o_ref[head] = y.astype(BF16)
            if head in G2_HEADS:
                y_scr[...] = y
                for res in range(DIL_G2):
                    d_ref[G2_HEADS.index(head), 0, :, res * HEAD_DIM:(res + 1) * HEAD_DIM] = (
                        y_scr[pl.ds(res, IN_TM // DIL_G2, stride=DIL_G2), :].astype(BF16))


def _in_proj(x2, gmix, w_qkv, gains, tables):
    t_tok = x2.shape[0]
    nseq = SEQ // IN_TM
    tab_spec = pl.BlockSpec((IN_TM, HEAD_DIM), lambda i: (i % nseq, 0))
    return pl.pallas_call(
        _in_proj_kernel,
        grid=(t_tok // IN_TM,),
        in_specs=[pl.BlockSpec((IN_TM, D_MODEL), lambda i: (i, 0)),
                  _resident(gmix.shape), _resident(w_qkv.shape), _resident(gains.shape),
                  tab_spec, tab_spec, tab_spec, tab_spec, tab_spec, tab_spec],
        out_specs=[pl.BlockSpec((N_QKV_HEADS, IN_TM, HEAD_DIM), lambda i: (0, i, 0)),
                   pl.BlockSpec((len(G2_HEADS), 1, IN_TM // DIL_G2, DIL_G2 * HEAD_DIM),
                                lambda i: (0, i // nseq, i % nseq, 0))],
        out_shape=[jax.ShapeDtypeStruct((N_QKV_HEADS, t_tok, HEAD_DIM), BF16),
                   jax.ShapeDtypeStruct((len(G2_HEADS), t_tok // SEQ, G2_STEPS, DIL_G2 * HEAD_DIM), BF16)],
        scratch_shapes=[pltpu.VMEM((IN_TM, HEAD_DIM), F32)],
        compiler_params=pltpu.CompilerParams(dimension_semantics=("arbitrary",),
                                             vmem_limit_bytes=VMEM_LIMIT),
        name="in_proj",
    )(x2, gmix, w_qkv, gains, *tables)


ATT_TQ = 256
NT = (((1,), (1,)), ((), ()))


def _attn_b_kernel(qa_ref, qb_ref, k_ref, v_ref, o_ref):
    for g, q_ref in enumerate((qa_ref, qb_ref)):
        k = k_ref[g]
        v = v_ref[g]
        for hh in range(4):
            s = lax.dot_general(q_ref[hh], k, NT, preferred_element_type=F32) * SCALE
            m = jnp.max(s, axis=-1, keepdims=True)
            p = jnp.exp(s - m)
            l = jnp.sum(p, axis=-1, keepdims=True)
            o = jnp.dot(p.astype(BF16), v, preferred_element_type=F32) / l
            col = (4 * g + hh) * HEAD_DIM
            o_ref[:, col:col + HEAD_DIM] = o.astype(BF16)


def _attn_b(qkv, n_batch):
    t_tok = qkv.shape[1]
    nq = SEQ // ATT_TQ

    def q_spec(g):
        return pl.BlockSpec((4, ATT_TQ, HEAD_DIM), lambda b, qi: (QB0 // 4 + g, b * nq + qi, 0))

    return pl.pallas_call(
        _attn_b_kernel,
        grid=(n_batch, nq),
        in_specs=[q_spec(0), q_spec(1),
                  pl.BlockSpec((2, SEQ, HEAD_DIM), lambda b, qi: (KB0 // 2, b, 0)),
                  pl.BlockSpec((2, SEQ, HEAD_DIM), lambda b, qi: (VB0 // 2, b, 0))],
        out_specs=pl.BlockSpec((ATT_TQ, 8 * HEAD_DIM), lambda b, qi: (b * nq + qi, 0)),
        out_shape=jax.ShapeDtypeStruct((t_tok, 8 * HEAD_DIM), BF16),
        compiler_params=pltpu.CompilerParams(
            dimension_semantics=("arbitrary", "arbitrary"), vmem_limit_bytes=VMEM_LIMIT),
        name="attn_b",
    )(qkv, qkv, qkv, qkv)


KEY_ALIGN = 64


def _key_window(dil):
    span = A_HALF * dil
    span = -(-span // KEY_ALIGN) * KEY_ALIGN
    width = min(SEQ, ATT_TQ + 2 * span)
    assert width % LANES == 0
    return width, span


def _attn_g2_kernel(q_ref, k_ref, v_ref, o_ref, lse_ref):
    i = lax.broadcasted_iota(jnp.int32, (G2_STEPS, G2_STEPS), 0)
    j = lax.broadcasted_iota(jnp.int32, (G2_STEPS, G2_STEPS), 1)
    band = jnp.abs(i - j) <= A_HALF
    for res in range(DIL_G2):
        cols = slice(res * HEAD_DIM, (res + 1) * HEAD_DIM)
        s = lax.dot_general(q_ref[0, 0, :, cols], k_ref[0, 0, :, cols], NT, preferred_element_type=F32) * SCALE
        s = jnp.where(band, s, NEG)
        m = jnp.max(s, axis=-1, keepdims=True)
        p = jnp.exp(s - m)
        l = jnp.sum(p, axis=-1, keepdims=True)
        o = jnp.dot(p.astype(BF16), v_ref[0, 0, :, cols], preferred_element_type=F32) / l
        rows = pl.ds(res, G2_STEPS, stride=DIL_G2)
        o_ref[rows, :] = o
        lse_ref[rows, :] = jnp.broadcast_to(m + jnp.log(l), (G2_STEPS, HEAD_DIM))


def _attn_g2(d16, n_batch):
    g4 = A_HEADS_PER_GROUP
    width = DIL_G2 * HEAD_DIM

    def spec(base):
        return pl.BlockSpec((1, 1, G2_STEPS, width), lambda b, hh: (base + hh, b, 0, 0))

    out_spec = pl.BlockSpec((SEQ, HEAD_DIM), lambda b, hh: (b, hh))
    out_shape = jax.ShapeDtypeStruct((n_batch * SEQ, g4 * HEAD_DIM), F32)
    return pl.pallas_call(
        _attn_g2_kernel,
        grid=(n_batch, g4),
        in_specs=[spec(0), spec(g4), spec(2 * g4)],
        out_specs=[out_spec, out_spec],
        out_shape=[out_shape, out_shape],
        compiler_params=pltpu.CompilerParams(dimension_semantics=("arbitrary", "arbitrary"),
                                             vmem_limit_bytes=VMEM_LIMIT),
        name="attn_g2",
    )(d16, d16, d16)


def _attn_a_kernel(q0_ref, q1_ref, k0_ref, k1_ref, v0_ref, v1_ref, o2_ref, lse2_ref, o_ref):
    q0pos = pl.program_id(1) * ATT_TQ
    for hh in range(A_HEADS_PER_GROUP):
        cols = slice(hh * HEAD_DIM, (hh + 1) * HEAD_DIM)
        scores, values = [], []
        for q_ref, k_ref, v_ref, dil in ((q0_ref, k0_ref, v0_ref, A_DILATIONS[0]),
                                         (q1_ref, k1_ref, v1_ref, A_DILATIONS[1])):
            width, span = _key_window(dil)
            start = pl.multiple_of(jnp.clip(q0pos - span, 0, SEQ - width), KEY_ALIGN)
            kk = k_ref[hh, pl.ds(start, width), :]
            vv = v_ref[hh, pl.ds(start, width), :]
            s = lax.dot_general(q_ref[hh], kk, NT, preferred_element_type=F32) * SCALE
            rel = (lax.broadcasted_iota(jnp.int32, (ATT_TQ, width), 1)
                   - lax.broadcasted_iota(jnp.int32, (ATT_TQ, width), 0)) + (start - q0pos)
            valid = jnp.abs(rel) <= A_HALF * dil
            if dil > 1:
                valid = jnp.logical_and(valid, (rel & (dil - 1)) == 0)
            scores.append(jnp.where(valid, s, NEG))
            values.append(vv)
        lse2 = lse2_ref[:, hh * HEAD_DIM:hh * HEAD_DIM + 1]
        m = functools.reduce(jnp.maximum, [jnp.max(s, axis=-1, keepdims=True) for s in scores] + [lse2])
        w2 = jnp.exp(lse2 - m)
        l = w2
        acc = o2_ref[:, cols] * w2
        for s, vv in zip(scores, values):
            p = jnp.exp(s - m)
            l = l + jnp.sum(p, axis=-1, keepdims=True)
            acc = acc + jnp.dot(p.astype(BF16), vv, preferred_element_type=F32)
        o_ref[:, cols] = (acc / l).astype(BF16)


def _attn_a(qkv, o2, lse2, n_batch):
    t_tok = qkv.shape[1]
    nq = SEQ // ATT_TQ
    g4 = A_HEADS_PER_GROUP
    assert all(_key_window(d)[0] < SEQ for d in A_DILATIONS[:2])

    def q_spec(g):
        return pl.BlockSpec((g4, ATT_TQ, HEAD_DIM), lambda b, qi: (QA0 // g4 + g, b * nq + qi, 0))

    def kv_spec(base, g):
        return pl.BlockSpec((g4, SEQ, HEAD_DIM), lambda b, qi: (base // g4 + g, b, 0))

    tile_spec = pl.BlockSpec((ATT_TQ, g4 * HEAD_DIM), lambda b, qi: (b * nq + qi, 0))
    return pl.pallas_call(
        _attn_a_kernel,
        grid=(n_batch, nq),
        in_specs=[q_spec(0), q_spec(1), kv_spec(KA0, 0), kv_spec(KA0, 1), kv_spec(VA0, 0), kv_spec(VA0, 1),
                  tile_spec, tile_spec],
        out_specs=tile_spec,
        out_shape=jax.ShapeDtypeStruct((t_tok, g4 * HEAD_DIM), BF16),
        compiler_params=pltpu.CompilerParams(
            dimension_semantics=("arbitrary", "arbitrary"), vmem_limit_bytes=VMEM_LIMIT),
        name="attn_a",
    )(*([qkv] * 6), o2, lse2)


MG_TM = 256
MG_CH = 512


def _sigmoid(z):
    return 1.0 / (1.0 + jnp.exp(-z))


def _merge_kernel(x_ref, oa_ref, ob_ref, gmix_ref, wg_ref, wa_ref, wb_ref, wo_ref, o_ref, m_scr):
    x = x_ref[...]
    ms = jnp.mean(x * x, axis=-1, keepdims=True)
    h = (x * lax.rsqrt(ms + EPS) * gmix_ref[...]).astype(BF16)
    oa = oa_ref[...]
    ob = ob_ref[...]
    for c in range(D_MODEL // MG_CH):
        lo, hi = c * MG_CH, (c + 1) * MG_CH
        ga = jnp.dot(h, wg_ref[:, lo:hi], preferred_element_type=F32)
        gb = jnp.dot(h, wg_ref[:, D_MODEL + lo:D_MODEL + hi], preferred_element_type=F32)
        pa = jnp.dot(oa, wa_ref[:, lo:hi], preferred_element_type=F32)
        pb = jnp.dot(ob, wb_ref[:, lo:hi], preferred_element_type=F32)
        m_scr[:, lo:hi] = (_sigmoid(ga) * pa + _sigmoid(gb) * pb).astype(BF16)
    o_ref[...] = x + jnp.dot(m_scr[...], wo_ref[...], preferred_element_type=F32)


def _resident(shape):
    nd = len(shape)
    return pl.BlockSpec(shape, lambda i: (0,) * nd, pipeline_mode=pl.Buffered(1))


def _merge(x2, oa, ob, gmix, w_g, w_a, w_b, w_o):
    t_tok = x2.shape[0]
    return pl.pallas_call(
        _merge_kernel,
        grid=(t_tok // MG_TM,),
        in_specs=[pl.BlockSpec((MG_TM, D_MODEL), lambda i: (i, 0)),
                  pl.BlockSpec((MG_TM, oa.shape[1]), lambda i: (i, 0)),
                  pl.BlockSpec((MG_TM, ob.shape[1]), lambda i: (i, 0)),
                  _resident(gmix.shape), _resident(w_g.shape), _resident(w_a.shape),
                  _resident(w_b.shape), _resident(w_o.shape)],
        out_specs=pl.BlockSpec((MG_TM, D_MODEL), lambda i: (i, 0)),
        out_shape=jax.ShapeDtypeStruct((t_tok, D_MODEL), F32),
        scratch_shapes=[pltpu.VMEM((MG_TM, D_MODEL), BF16)],
        compiler_params=pltpu.CompilerParams(dimension_semantics=("arbitrary",),
                                             vmem_limit_bytes=VMEM_LIMIT),
        name="merge",
    )(x2, oa, ob, gmix, w_g, w_a, w_b, w_o)


RT_TM = 256


def _top16_rows(s, key_f, val_ref, idx_ref):
    for r in range(PEER_TOPK):
        m = jnp.max(s, axis=0, keepdims=True)
        idx = jnp.min(jnp.where(s == m, key_f, float(PEER_NKEYS)), axis=0, keepdims=True)
        val_ref[r:r + 1, :] = m
        idx_ref[r:r + 1, :] = idx
        s = jnp.where(key_f == idx, -jnp.inf, s)


PERM8 = (0, 4, 2, 6, 1, 5, 3, 7)


def _coef_row(q):
    return (q // 8) * 8 + PERM8[q % 8]


def _peer_route_kernel(x_ref, gffn_ref, wpq_ref, sk_ref, e_ref, g_ref,
                       q_scr, v1_scr, i1_scr, v2_scr, i2_scr, cv_scr, ce_scr, et_scr):
    x = x_ref[...]
    ms = jnp.mean(x * x, axis=-1, keepdims=True)
    h = (x * lax.rsqrt(ms + EPS) * gffn_ref[...]).astype(BF16)
    q = jnp.dot(h, wpq_ref[...], preferred_element_type=F32)
    for hc in range(2 * PEER_HEADS):
        q_scr[hc] = q[:, hc * LANES:(hc + 1) * LANES].astype(BF16)

    key_f = lax.broadcasted_iota(jnp.int32, (PEER_NKEYS, LANES), 0).astype(F32)
    sub = lax.broadcasted_iota(jnp.int32, (8, LANES), 0)
    zero8 = jnp.zeros_like(sub)
    ca = jnp.concatenate([zero8, zero8, zero8 + 1, zero8 + 2, 3 + (sub >> 2), 5 + (sub >> 1), 8 + sub], axis=0)
    cb = jnp.concatenate([sub, sub + 8, sub, sub, sub & 3, sub & 1, zero8], axis=0)
    pairs_ok = jnp.concatenate([zero8 == 0] * 5 + [sub < 6, zero8 == 0], axis=0)
    cand_ok = jnp.logical_and((ca + 1) * (cb + 1) <= PEER_TOPK, pairs_ok)
    flat_f = jnp.where(cand_ok, ca * PEER_TOPK + cb, -1).astype(F32)

    def cand_tiles(t1, t2, combine):
        lo2, hi2 = t2[0:8], t2[8:16]
        return jnp.concatenate([
            combine(t1[0:1], lo2), combine(t1[0:1], hi2), combine(t1[1:2], lo2), combine(t1[2:3], lo2),
            combine(jnp.where(sub < 4, t1[3:4], t1[4:5]), jnp.where(sub < 4, lo2, pltpu.roll(lo2, 4, 0))),
            combine(jnp.where(sub < 2, t1[5:6], jnp.where(sub < 4, t1[6:7], t1[7:8])),
                    jnp.where((sub & 1) == 0, t2[0:1], t2[1:2])),
            combine(t1[8:16], t2[0:1])], axis=0)

    def head_body(hd, carry):
        for lh in range(RT_TM // LANES):
            for c, (val_ref, idx_ref) in enumerate(((v1_scr, i1_scr), (v2_scr, i2_scr))):
                s = lax.dot_general(sk_ref[hd * 2 + c], q_scr[hd * 2 + c, pl.ds(lh * LANES, LANES), :],
                                    NT, preferred_element_type=F32)
                _top16_rows(s, key_f, val_ref, idx_ref)
            v1, i1 = v1_scr[...], i1_scr[...]
            v2, i2 = v2_scr[...], i2_scr[...]
            cand = jnp.where(cand_ok, cand_tiles(v1, v2, lambda x, y: x + y), -jnp.inf)
            cexp = cand_tiles(i1, i2, lambda x, y: x * float(PEER_NKEYS) + y)
            for r in range(PEER_TOPK):
                m = jnp.max(cand, axis=0, keepdims=True)
                sel = jnp.min(jnp.where(cand == m, flat_f, 1e9), axis=0, keepdims=True)
                hit = flat_f == sel
                cv_scr[_coef_row(r):_coef_row(r) + 1, :] = m
                ce_scr[r:r + 1, :] = jnp.max(jnp.where(hit, cexp, -1.0), axis=0, keepdims=True)
                cand = jnp.where(hit, -jnp.inf, cand)
            cv = cv_scr[...]
            w = jnp.exp(cv - cv[0:1])
            w = w / jnp.sum(w, axis=0, keepdims=True)
            rows = pl.ds(pl.multiple_of(hd * PEER_TOPK, PEER_TOPK), PEER_TOPK)
            et_scr[rows, lh * LANES:(lh + 1) * LANES] = ce_scr[...]
            g_ref[rows, lh * LANES:(lh + 1) * LANES] = w
        return carry

    lax.fori_loop(0, PEER_HEADS, head_body, 0)
    e_ref[...] = et_scr[...].T.astype(jnp.int32)


def _peer_route(x1, gffn, w_pq, sk):
    t_tok = x1.shape[0]
    small = [pltpu.VMEM((PEER_TOPK, LANES), F32) for _ in range(6)]
    return pl.pallas_call(
        _peer_route_kernel,
        grid=(t_tok // RT_TM,),
        in_specs=[pl.BlockSpec((RT_TM, D_MODEL), lambda i: (i, 0)),
                  _resident(gffn.shape), _resident(w_pq.shape), _resident(sk.shape)],
        out_specs=[pl.BlockSpec((RT_TM, PEER_SEL), lambda i: (i, 0)),
                   pl.BlockSpec((PEER_SEL, RT_TM), lambda i: (0, i))],
        out_shape=[jax.ShapeDtypeStruct((t_tok, PEER_SEL), jnp.int32),
                   jax.ShapeDtypeStruct((PEER_SEL, t_tok), F32)],
        scratch_shapes=[pltpu.VMEM((2 * PEER_HEADS, RT_TM, LANES), BF16)] + small
                       + [pltpu.VMEM((PEER_SEL, RT_TM), F32)],
        compiler_params=pltpu.CompilerParams(dimension_semantics=("arbitrary",),
                                             vmem_limit_bytes=VMEM_LIMIT),
        name="peer_route",
    )(x1, gffn, w_pq, sk)


MX_TB = 128
MX_SLOTS = 16
MX_AHEAD = 12


def _merge8(ps, sub):
    lo4, lo2, lo1 = (sub & 4) == 0, (sub & 2) == 0, (sub & 1) == 0
    q = [jnp.where(lo4, ps[2 * i], ps[2 * i + 1])
         + pltpu.roll(jnp.where(lo4, ps[2 * i + 1], ps[2 * i]), 4, 0) for i in range(4)]
    r = [jnp.where(lo2, q[2 * i] + pltpu.roll(q[2 * i], 6, 0), q[2 * i + 1] + pltpu.roll(q[2 * i + 1], 2, 0))
         for i in range(2)]
    return jnp.where(lo1, r[0] + pltpu.roll(r[0], 7, 0), r[1] + pltpu.roll(r[1], 1, 0))


def _peer_mix_kernel(e_ref, en_ref, x_ref, g_ref, gffn_ref, tab_ref, y_ref, buf, sem, h_scr, m_scr,
                     c_scr):
    step_i = pl.program_id(0)
    x = x_ref[...].reshape(MX_TB // 8, ROW_CHUNKS, 8, LANES)
    ss = jnp.sum(jnp.sum(x * x, axis=3, keepdims=True), axis=1, keepdims=True)
    h = x * lax.rsqrt(ss * (1.0 / D_MODEL) + EPS) * gffn_ref[...]
    h_scr[...] = h.reshape(MX_TB * ROW_CHUNKS, LANES)

    sub = lax.broadcasted_iota(jnp.int32, (8, LANES), 0)
    lane = lax.broadcasted_iota(jnp.int32, (PEER_SEL, MX_TB), 1)

    def row_copy(idx_ref, tok, k, slot):
        return pltpu.make_async_copy(tab_ref.at[idx_ref[tok, k]], buf.at[slot, k], sem.at[slot])

    def wait_rows(slot):
        pltpu.make_async_copy(tab_ref.at[pl.ds(0, PEER_SEL)], buf.at[slot], sem.at[slot]).wait()

    def step(s, r, idx_ref, itok, do_u, do_c, do_v):
        slot_v, slot_u, slot_i = r % MX_SLOTS, (r + 2) % MX_SLOTS, (r + MX_AHEAD) % MX_SLOTS
        par = r % 2
        if do_u:
            wait_rows(slot_u)
            hrow = h_scr[_row_slice(s + 2), :]
            hlo, hhi = hrow[0:8], hrow[8:16]
        if do_c:
            a = jnp.sum(m_scr[1 - par], axis=1, keepdims=True)
            gate = jnp.sum(jnp.where(lane == s + 1, g_ref[...], 0.0), axis=1, keepdims=True)
            coef = 0.5 * a * (1.0 + lax.erf(a * INV_SQRT2)) * gate
        zero = jnp.zeros((ROW_CHUNKS, LANES), F32)
        accs = [zero, zero, zero, zero]
        merged = []
        if idx_ref is not None:
            slot_i = slot_i + (idx_ref[itok, 0] >> 31)
        for j in range(PEER_SEL // 8):
            parts = []
            for kk in range(8):
                q = 8 * j + kk
                if idx_ref is not None:
                    row_copy(idx_ref, itok, q, slot_i).start(priority=q % 2)
                if do_u:
                    u = buf[slot_u, q, 0:ROW_CHUNKS, :].astype(F32)
                    parts.append(u[0:8] * hlo + u[8:16] * hhi)
                if do_v:
                    v = buf[slot_v, q, ROW_CHUNKS:2 * ROW_CHUNKS, :].astype(F32)
                    accs[kk % 4] = accs[kk % 4] + c_scr[par, pl.ds(_coef_row(q), 1), :] * v
            if do_u:
                merged.append(_merge8(parts, sub))
        if do_v:
            rows = _row_slice(s)
            y_ref[rows, :] = x_ref[rows, :] + ((accs[0] + accs[1]) + (accs[2] + accs[3]))
        if do_u:
            for j in range(PEER_SEL // 8):
                m_scr[par, 8 * j:8 * j + 8, :] = merged[j]
        if do_c:
            c_scr[1 - par] = jnp.broadcast_to(coef, (PEER_SEL, LANES))

    @pl.when(step_i == 0)
    def _():
        for tok in range(MX_AHEAD):
            for q in range(PEER_SEL):
                row_copy(e_ref, tok, q, tok).start(priority=q % 2)

    step(-2, MX_SLOTS - 2, None, None, True, False, False)
    step(-1, MX_SLOTS - 1, None, None, True, True, False)

    def body(it, carry):
        for r in range(MX_SLOTS):
            s = it * MX_SLOTS + r
            step(s, r, e_ref, s + MX_AHEAD, True, True, True)
        return carry

    lax.fori_loop(0, MX_TB // MX_SLOTS - 1, body, 0)
    for s in range(MX_TB - MX_SLOTS, MX_TB):
        nxt = s + MX_AHEAD - MX_TB
        src, tok = (e_ref, s + MX_AHEAD) if nxt < 0 else (en_ref, nxt)
        step(s, s % MX_SLOTS, src, tok, s + 2 < MX_TB, s + 1 < MX_TB, True)

    @pl.when(step_i == pl.num_programs(0) - 1)
    def _():
        for tok in range(MX_AHEAD):
            wait_rows(tok)


def _peer_mix(x1, e_t, g_t, gffn, table):
    t_tok = x1.shape[0]
    assert MX_SLOTS & (MX_SLOTS - 1) == 0 and MX_TB % MX_SLOTS == 0 and MX_AHEAD < MX_SLOTS
    rows = MX_TB * ROW_CHUNKS
    last_blk = t_tok // MX_SLOTS - 1
    y_rows = pl.pallas_call(
        _peer_mix_kernel,
        grid=(t_tok // MX_TB,),
        in_specs=[pl.BlockSpec((MX_TB, PEER_SEL), lambda i: (i, 0), memory_space=pltpu.SMEM),
                  pl.BlockSpec((MX_SLOTS, PEER_SEL),
                               lambda i: (jnp.minimum((i + 1) * (MX_TB // MX_SLOTS), last_blk), 0),
                               memory_space=pltpu.SMEM),
                  pl.BlockSpec((rows, LANES), lambda i: (i, 0)),
                  pl.BlockSpec((PEER_SEL, MX_TB), lambda i: (0, i)),
                  pl.BlockSpec((ROW_CHUNKS, 1, LANES), lambda i: (0, 0, 0)),
                  pl.BlockSpec(memory_space=pl.ANY)],
        out_specs=pl.BlockSpec((rows, LANES), lambda i: (i, 0)),
        out_shape=jax.ShapeDtypeStruct((t_tok * ROW_CHUNKS, LANES), F32),
        scratch_shapes=[pltpu.VMEM((MX_SLOTS, PEER_SEL, 2 * ROW_CHUNKS, LANES), BF16),
                        pltpu.SemaphoreType.DMA((MX_SLOTS,)),
                        pltpu.VMEM((rows, LANES), F32),
                        pltpu.VMEM((2, PEER_SEL, LANES), F32),
                        pltpu.VMEM((2, PEER_SEL, LANES), F32)],
        compiler_params=pltpu.CompilerParams(dimension_semantics=("arbitrary",),
                                             vmem_limit_bytes=VMEM_LIMIT),
        name="peer_mix",
    )(e_t, e_t, _to_rows(x1), g_t, gffn.reshape(ROW_CHUNKS, 1, LANES), table)
    return _from_rows(y_rows, t_tok)


def _prepare(g_mix, w_in, qn_a, kn_a, qn_b, kn_b, w_br_a, w_br_b, w_out, g_ffn, w_pq, sub_keys,
             u_emb, v_emb):
    return dict(
        gmix=g_mix.reshape(1, D_MODEL),
        w_qkv=w_in[:, :QKV_WIDTH].astype(BF16),
        w_g=w_in[:, QKV_WIDTH:].astype(BF16),
        gains=jnp.stack([qn_a, kn_a, qn_b, kn_b]),
        tables=_rope_tables(),
        w_a=w_br_a.astype(BF16), w_b=w_br_b.astype(BF16), w_o=w_out.astype(BF16),
        gffn=g_ffn.reshape(1, D_MODEL),
        w_pq=w_pq.astype(BF16),
        sk=sub_keys.reshape(2 * PEER_HEADS, PEER_NKEYS, LANES).astype(BF16),
        table=_pack_table(u_emb, v_emb),
    )


def _layer(x, p):
    n_batch, length, d = x.shape
    assert length == SEQ and d == D_MODEL
    x2 = x.reshape(n_batch * length, d)
    qkv, d16 = _in_proj(x2, p["gmix"], p["w_qkv"], p["gains"], p["tables"])
    ob = _attn_b(qkv, n_batch)
    o2, lse2 = _attn_g2(d16, n_batch)
    oa = _attn_a(qkv, o2, lse2, n_batch)
    x1 = _merge(x2, oa, ob, p["gmix"], p["w_g"], p["w_a"], p["w_b"], p["w_o"])
    e_t, g_t = _peer_route(x1, p["gffn"], p["w_pq"], p["sk"])
    y = _peer_mix(x1, e_t, g_t, p["gffn"], p["table"])
    return y.reshape(n_batch, length, d)


def kernel(x_prompt, x_sample, g_mix, w_in, qn_a, kn_a, qn_b, kn_b, w_br_a, w_br_b, w_out, g_ffn,
           w_pq, sub_keys, u_emb, v_emb):
    y_prompt, y_sample = x_prompt, x_sample
    for l in range(g_mix.shape[0]):
        p = _prepare(g_mix[l], w_in[l], qn_a[l], kn_a[l], qn_b[l], kn_b[l], w_br_a[l], w_br_b[l],
                     w_out[l], g_ffn[l], w_pq[l], sub_keys[l], u_emb[l], v_emb[l])
        y_prompt = _layer(y_prompt, p)
        y_sample = _layer(y_sample, p)
    return (y_prompt, y_sample)
```
